```python
import jax, jax.numpy as jnp
from jax import lax
import numpy as np

D_MODEL = 2048
BATCH = 16
SEQ = 256
DEPTH = 4
DEC_BATCH = 8
DEC_SEQ = 4096
PAST_LEN = 512

GRID_W = 64
HEAD_DIM = 128
N_HEADS = D_MODEL // 256
N_KV_HEADS = D_MODEL // 1024
Q_PER_KV = N_HEADS // N_KV_HEADS
ATTN_WIDTH = N_HEADS * HEAD_DIM
KV_WIDTH = N_KV_HEADS * HEAD_DIM
POOL_WINDOWS = (2, 4, 8, 16)
N_POOL_GROUPS = len(POOL_WINDOWS)
POOL_GROUP_DIM = D_MODEL // 16
POOL_WIDTH = N_POOL_GROUPS * POOL_GROUP_DIM
N_FOURIER_GROUPS = 4
FOURIER_GROUP_DIM = D_MODEL // 16
FOURIER_WIDTH = N_FOURIER_GROUPS * FOURIER_GROUP_DIM
N_BRANCHES = 3
IN_WIDTH = ATTN_WIDTH + 2 * KV_WIDTH + POOL_WIDTH + FOURIER_WIDTH + N_BRANCHES * D_MODEL
D_FF = 5632
N_EXPERTS = 8
TOP_K = 2
D_FF_EXPERT = 7168
N_DENSE = (DEPTH + 1) // 2
N_MOE = DEPTH // 2
Q_BLOCK = 128
MOE_BLOCK = 128
ROPE_THETA = 10000.0
EPS = 1e-6

kernel_name = 'hybrid_gqa_pool_fourier_diffusion_step'


def rms_norm(x, g):
    xf = x.astype(jnp.float32)
    y = xf * lax.rsqrt(jnp.mean(xf * xf, axis=-1, keepdims=True) + EPS)
    return (y * g.astype(jnp.float32)).astype(x.dtype)


def axial_rope_angles(n_rows):
    rows = jnp.repeat(jnp.arange(n_rows, dtype=jnp.float32), GRID_W)
    cols = jnp.tile(jnp.arange(GRID_W, dtype=jnp.float32), n_rows)
    n_freq = HEAD_DIM // 4
    inv_freq = jnp.power(ROPE_THETA, -jnp.arange(n_freq, dtype=jnp.float32) / n_freq)
    return rows[:, None] * inv_freq, cols[:, None] * inv_freq


def rotate_pairs(x, ang):
    xf = x.astype(jnp.float32)
    x1, x2 = jnp.split(xf, 2, axis=-1)
    cos = jnp.cos(ang)[None, :, None, :]
    sin = jnp.sin(ang)[None, :, None, :]
    return jnp.concatenate([x1 * cos - x2 * sin, x2 * cos + x1 * sin], axis=-1).astype(x.dtype)


def axial_rope(x, ang_row, ang_col):
    xr, xc = jnp.split(x, 2, axis=-1)
    return jnp.concatenate([rotate_pairs(xr, ang_row), rotate_pairs(xc, ang_col)], axis=-1)


def block_attention(q, k, v):
    B, S = q.shape[0], q.shape[1]
    nb = S // Q_BLOCK
    qb = q.reshape(B, nb, Q_BLOCK, N_KV_HEADS, Q_PER_KV, HEAD_DIM).transpose(1, 0, 2, 3, 4, 5)
    scale = HEAD_DIM ** -0.5

    def one_block(qblk):
        s = jnp.einsum('bqkgd,btkd->bkgqt', qblk, k, preferred_element_type=jnp.float32) * scale
        p = jax.nn.softmax(s, axis=-1).astype(v.dtype)
        return jnp.einsum('bkgqt,btkd->bqkgd', p, v)

    o = lax.map(one_block, qb)
    return o.transpose(1, 0, 2, 3, 4, 5).reshape(B, S, ATTN_WIDTH)


def multiscale_pool(u, w_pool_map, pool_scale):
    B, S = u.shape[0], u.shape[1]
    uf = u.reshape(B, S, N_POOL_GROUPS, POOL_GROUP_DIM).astype(jnp.float32)
    csum = jnp.concatenate([jnp.zeros((B, 1, N_POOL_GROUPS, POOL_GROUP_DIM), jnp.float32),
                            lax.cumsum(uf, axis=1)], axis=1)
    t = np.arange(S)
    outs = []
    for gi, w in enumerate(POOL_WINDOWS):
        lo = np.clip(t - w // 2, 0, S)
        hi = np.clip(t - w // 2 + w, 0, S)
        cnt = jnp.asarray((hi - lo).astype(np.float32))
        cg = csum[:, :, gi]
        mean = (cg[:, hi] - cg[:, lo]) / cnt[None, :, None]
        outs.append(mean - uf[:, :, gi])
    pooled = jnp.stack(outs, axis=2).astype(u.dtype)
    mixed = jnp.einsum('bsgc,gcd->bsgd', pooled, w_pool_map)
    return mixed.reshape(B, S, POOL_WIDTH) * pool_scale


def fourier_mix(u):
    B, S = u.shape[0], u.shape[1]
    ug = u.reshape(B, S, N_FOURIER_GROUPS, FOURIER_GROUP_DIM).astype(jnp.float32)
    f = jnp.fft.fft2(ug, axes=(1, 3), norm='ortho').real
    return f.reshape(B, S, FOURIER_WIDTH).astype(u.dtype)


def token_mixer(h, w_in, g_q, g_k, w_pool_map, pool_scale, w_attn_o, w_pool_o, w_four_o, w_out,
                ctx_k, ctx_v, rope):
    B, S = h.shape[0], h.shape[1]
    z = jnp.matmul(h, w_in)
    o1 = ATTN_WIDTH
    o2 = o1 + KV_WIDTH
    o3 = o2 + KV_WIDTH
    o4 = o3 + POOL_WIDTH
    o5 = o4 + FOURIER_WIDTH
    q, k, v, u_pool, u_four, gates = jnp.split(z, [o1, o2, o3, o4, o5], axis=-1)
    q = rms_norm(q.reshape(B, S, N_HEADS, HEAD_DIM), g_q)
    k = rms_norm(k.reshape(B, S, N_KV_HEADS, HEAD_DIM), g_k)
    v = v.reshape(B, S, N_KV_HEADS, HEAD_DIM)
    if rope is None:
        attn = block_attention(q, k, v)
    else:
        q_r = axial_rope(q, rope[0], rope[1])
        k_r = axial_rope(k, rope[0], rope[1])
        attn = block_attention(q_r, jnp.concatenate([ctx_k, k_r], axis=1),
                               jnp.concatenate([ctx_v, v], axis=1))
    pool = multiscale_pool(u_pool, w_pool_map, pool_scale)
    four = fourier_mix(u_four)
    g_attn, g_pool, g_four = jnp.split(jax.nn.sigmoid(gates), N_BRANCHES, axis=-1)
    merged = (g_attn * jnp.matmul(attn, w_attn_o) + g_pool * jnp.matmul(pool, w_pool_o)
              + g_four * jnp.matmul(four, w_four_o))
    return jnp.matmul(merged, w_out), k, v


def swiglu(x, wg, wu, wd):
    return jnp.matmul(jax.nn.silu(jnp.matmul(x, wg)) * jnp.matmul(x, wu), wd)


def moe_swiglu(x, w_router, w_gate, w_up, w_down):
    shp = x.shape
    h = x.reshape(-1, shp[-1])
    N = h.shape[0]
    logits = jnp.matmul(h, w_router, preferred_element_type=jnp.float32)
    top_logit, top_idx = lax.top_k(logits, TOP_K)
    top_w = jax.nn.softmax(top_logit, axis=-1)
    A = N * TOP_K
    flat_e = top_idx.reshape(A)
    flat_tok = jnp.repeat(jnp.arange(N, dtype=jnp.int32), TOP_K)
    flat_w = top_w.reshape(A)
    order = jnp.argsort(flat_e)
    se = flat_e[order]
    st = flat_tok[order]
    sw = flat_w[order]
    counts = jnp.bincount(flat_e, length=N_EXPERTS)
    start = jnp.cumsum(counts) - counts
    padded = (counts + MOE_BLOCK - 1) // MOE_BLOCK * MOE_BLOCK
    pad_end = jnp.cumsum(padded)
    pad_start = pad_end - padded
    dest = pad_start[se] + jnp.arange(A) - start[se]
    n_blocks = -(-A // MOE_BLOCK) + N_EXPERTS
    P = n_blocks * MOE_BLOCK
    slot_tok = jnp.zeros((P,), jnp.int32).at[dest].set(st)
    block_start = jnp.arange(n_blocks) * MOE_BLOCK
    block_expert = jnp.minimum(jnp.searchsorted(pad_end, block_start, side='right'), N_EXPERTS - 1)
    xb = h[slot_tok].reshape(n_blocks, MOE_BLOCK, shp[-1])

    def expert_block(args):
        xblk, e = args
        return swiglu(xblk, w_gate[e], w_up[e], w_down[e])

    yb = lax.map(expert_block, (xb, block_expert)).reshape(P, shp[-1])
    contrib = yb[dest] * sw[:, None].astype(yb.dtype)
    out = jnp.zeros_like(h).at[st].add(contrib)
    return out.reshape(shp)


def setup_inputs(seed: int = 0) -> dict:
    key = jax.random.key(seed)
    ks = jax.random.split(key, 32)
    f32 = jnp.float32

    def nrm(k, shape, fan_in, gain=1.0):
        return jax.random.normal(k, shape, f32) * (gain * fan_in ** -0.5)

    def gain_vec(k, shape):
        return 1.0 + 0.05 * jax.random.normal(k, shape, f32)

    return {
        'x_prompt': jax.random.normal(ks[0], (BATCH, SEQ, D_MODEL), f32),
        'x_sample': jax.random.normal(ks[1], (DEC_BATCH, DEC_SEQ, D_MODEL), f32),
        'cache_k': jax.random.normal(ks[2], (DEC_BATCH, DEPTH, PAST_LEN, N_KV_HEADS, HEAD_DIM), f32),
        'cache_v': jax.random.normal(ks[3], (DEC_BATCH, DEPTH, PAST_LEN, N_KV_HEADS, HEAD_DIM), f32),
        'c': jax.random.normal(ks[4], (DEC_BATCH, D_MODEL), f32),
        'c_ctx': jax.random.normal(ks[5], (D_MODEL,), f32),
        'w_mod': nrm(ks[6], (DEPTH, D_MODEL, 6 * D_MODEL), D_MODEL, 0.2),
        'b_mod': 0.01 * jax.random.normal(ks[7], (DEPTH, 6 * D_MODEL), f32),
        'g_pre_mix': gain_vec(ks[8], (DEPTH, D_MODEL)),
        'g_post_mix': gain_vec(ks[9], (DEPTH, D_MODEL)),
        'g_pre_ffn': gain_vec(ks[10], (DEPTH, D_MODEL)),
        'g_post_ffn': gain_vec(ks[11], (DEPTH, D_MODEL)),
        'w_in': nrm(ks[12], (DEPTH, D_MODEL, IN_WIDTH), D_MODEL),
        'g_q': gain_vec(ks[13], (DEPTH, HEAD_DIM)),
        'g_k': gain_vec(ks[14], (DEPTH, HEAD_DIM)),
        'w_pool_map': nrm(ks[15], (DEPTH, N_POOL_GROUPS, POOL_GROUP_DIM, POOL_GROUP_DIM), POOL_GROUP_DIM),
        'pool_scale': gain_vec(ks[16], (DEPTH, POOL_WIDTH)),
        'w_attn_o': nrm(ks[17], (DEPTH, ATTN_WIDTH, D_MODEL), ATTN_WIDTH),
        'w_pool_o': nrm(ks[18], (DEPTH, POOL_WIDTH, D_MODEL), POOL_WIDTH),
        'w_four_o': nrm(ks[19], (DEPTH, FOURIER_WIDTH, D_MODEL), FOURIER_WIDTH),
        'w_out': nrm(ks[20], (DEPTH, D_MODEL, D_MODEL), D_MODEL),
        'w_ffn_gate': nrm(ks[21], (N_DENSE, D_MODEL, D_FF), D_MODEL),
        'w_ffn_up': nrm(ks[22], (N_DENSE, D_MODEL, D_FF), D_MODEL),
        'w_ffn_down': nrm(ks[23], (N_DENSE, D_FF, D_MODEL), D_FF),
        'w_router': nrm(ks[24], (N_MOE, D_MODEL, N_EXPERTS), D_MODEL),
        'w_exp_gate': nrm(ks[25], (N_MOE, N_EXPERTS, D_MODEL, D_FF_EXPERT), D_MODEL),
        'w_exp_up': nrm(ks[26], (N_MOE, N_EXPERTS, D_MODEL, D_FF_EXPERT), D_MODEL),
        'w_exp_down': nrm(ks[27], (N_MOE, N_EXPERTS, D_FF_EXPERT, D_MODEL), D_FF_EXPERT),
    }


def reference(x_prompt, x_sample, cache_k, cache_v, c, c_ctx, w_mod, b_mod, g_pre_mix, g_post_mix,
              g_pre_ffn, g_post_ffn, w_in, g_q, g_k, w_pool_map, pool_scale, w_attn_o, w_pool_o,
              w_four_o, w_out, w_ffn_gate, w_ffn_up, w_ffn_down, w_router, w_exp_gate, w_exp_up,
              w_exp_down):
    n_rows = x_sample.shape[1] // GRID_W
    ang_row, ang_col = axial_rope_angles(n_rows)

    def apply_layer(x, cond, l, ctx_k, ctx_v, rope):
        mod = jnp.matmul(jax.nn.silu(cond), w_mod[l]) + b_mod[l]
        sh_m, sc_m, gt_m, sh_f, sc_f, gt_f = jnp.split(mod[..., None, :], 6, axis=-1)
        h = rms_norm(x, g_pre_mix[l]) * (1 + sc_m) + sh_m
        y, k, v = token_mixer(h, w_in[l], g_q[l], g_k[l], w_pool_map[l], pool_scale[l], w_attn_o[l],
                              w_pool_o[l], w_four_o[l], w_out[l], ctx_k, ctx_v, rope)
        x = x + gt_m * rms_norm(y, g_post_mix[l])
        h = rms_norm(x, g_pre_ffn[l]) * (1 + sc_f) + sh_f
        i = l // 2
        if l % 2 == 0:
            y = swiglu(h, w_ffn_gate[i], w_ffn_up[i], w_ffn_down[i])
        else:
            y = moe_swiglu(h, w_router[i], w_exp_gate[i], w_exp_up[i], w_exp_down[i])
        x = x + gt_f * rms_norm(y, g_post_ffn[l])
        return x, k, v

    xp = x_prompt
    xs = x_sample
    ks_ctx = []
    vs_ctx = []
    for l in range(DEPTH):
        xp, kc, vc = apply_layer(xp, c_ctx, l, None, None, None)
        ks_ctx.append(kc)
        vs_ctx.append(vc)
        xs, _, _ = apply_layer(xs, c, l, cache_k[:, l], cache_v[:, l], (ang_row, ang_col))
    new_k = jnp.stack(ks_ctx, axis=1)
    new_v = jnp.stack(vs_ctx, axis=1)
    return (xp, xs, new_k, new_v)
```

```python
import functools
import math

import jax
import jax.numpy as jnp
from jax import lax
from jax.experimental import pallas as pl
from jax.experimental.pallas import tpu as pltpu

F32 = jnp.float32
BF16 = jnp.bfloat16
I32 = jnp.int32

LANES = 128
GRID_W = 64
ROPE_THETA = 10000.0
EPS = 1e-6
TOP_K = 2
POOL_WINDOWS = (2, 4, 8, 16)
POOL_HALF_MAX = max(POOL_WINDOWS) // 2
VMEM_LIMIT = 56 * 1024 * 1024


def _cp(sem, vmem=VMEM_LIMIT):
    return pltpu.CompilerParams(dimension_semantics=sem, vmem_limit_bytes=vmem)


def _rms(x, g):
    return x * lax.rsqrt(jnp.mean(x * x, axis=-1, keepdims=True) + EPS) * g


def _silu(x):
    return x / (1.0 + jnp.exp(-x))


def _dot(a, b):
    return jnp.dot(a, b, preferred_element_type=F32)


def _resident(shape, index_map):
    return pl.BlockSpec(shape, index_map, pipeline_mode=pl.Buffered(1))


def _mod_kernel(c_ref, w_ref, b_ref, o_ref):
    s = _silu(c_ref[...])
    o_ref[0] = jnp.dot(s, w_ref[0], preferred_element_type=F32,
                       precision=lax.Precision.HIGHEST) + b_ref[0]


def _modulation(cond, w_mod, b_mod):
    L, D, N = w_mod.shape
    R = cond.shape[0]
    tn = 1024
    return pl.pallas_call(
        _mod_kernel,
        grid=(L, N // tn),
        in_specs=[pl.BlockSpec((R, D), lambda l, j: (0, 0)),
                  pl.BlockSpec((1, D, tn), lambda l, j: (l, 0, j)),
                  pl.BlockSpec((1, 1, tn), lambda l, j: (l, 0, j))],
        out_specs=pl.BlockSpec((1, R, tn), lambda l, j: (l, 0, j)),
        out_shape=jax.ShapeDtypeStruct((L, R, N), F32),
        compiler_params=_cp(("arbitrary", "arbitrary")),
        name="modulation",
    )(cond, w_mod, b_mod.reshape(L, 1, N))


def _norm_mod_kernel(x_ref, g_ref, mod_ref, h_ref, *, D):
    mod = mod_ref[0]
    y = _rms(x_ref[...], g_ref[...])
    h_ref[...] = (y * (1.0 + mod[:, D:2 * D]) + mod[:, 0:D]).astype(BF16)


def _norm_mod(x, g, mod, cfg):
    T, D = x.shape
    TM = cfg["TM"]
    return pl.pallas_call(
        functools.partial(_norm_mod_kernel, D=D),
        grid=(T // TM,),
        in_specs=[pl.BlockSpec((TM, D), lambda i: (i, 0)),
                  pl.BlockSpec((1, D), lambda i: (0, 0)),
                  pl.BlockSpec((1, 1, 6 * D), lambda i: (cfg["mod_row"](i), 0, 0))],
        out_specs=pl.BlockSpec((TM, D), lambda i: (i, 0)),
        out_shape=jax.ShapeDtypeStruct((T, D), BF16),
        compiler_params=_cp(("arbitrary",)),
        name="norm_mod",
    )(x, g, mod)


def _qkv_kernel(h_ref, w_ref, gq_ref, gk_ref, cos_ref, sin_ref,
                q_ref, kb_ref, vb_ref, kf_ref, vf_ref, *, NH, KV, scale):
    acc = _dot(h_ref[...], w_ref[...])
    cos = cos_ref[...]
    sin = sin_ref[...]
    lane = lax.broadcasted_iota(I32, cos.shape, 1)
    first_half = (lane % (LANES // 2)) < (LANES // 4)

    def rope(y):
        partner = jnp.where(first_half, pltpu.roll(y, LANES - LANES // 4, axis=1),
                            pltpu.roll(y, LANES // 4, axis=1))
        return y * cos + partner * sin

    for hd in range(NH):
        z = acc[:, hd * LANES:(hd + 1) * LANES]
        q_ref[:, hd * LANES:(hd + 1) * LANES] = (rope(_rms(z, gq_ref[...])) * scale).astype(BF16)
    for hd in range(KV):
        z = acc[:, (NH + hd) * LANES:(NH + hd + 1) * LANES]
        y = _rms(z, gk_ref[...])
        kf_ref[:, hd * LANES:(hd + 1) * LANES] = y
        kb_ref[:, hd * LANES:(hd + 1) * LANES] = rope(y).astype(BF16)
    v = acc[:, (NH + KV) * LANES:(NH + 2 * KV) * LANES]
    vf_ref[...] = v
    vb_ref[...] = v.astype(BF16)


def _qkv(h, w, gq, gk, cos_tab, sin_tab, cfg):
    T, D = h.shape
    TM, NH, KV = cfg["TM"], cfg["NH"], cfg["KV"]
    AW, KW = NH * LANES, KV * LANES
    row = lambda i: (i, 0)
    return pl.pallas_call(
        functools.partial(_qkv_kernel, NH=NH, KV=KV, scale=float(LANES) ** -0.5),
        grid=(T // TM,),
        in_specs=[pl.BlockSpec((TM, D), row),
                  _resident((D, AW + 2 * KW), lambda i: (0, 0)),
                  pl.BlockSpec((1, LANES), lambda i: (0, 0)),
                  pl.BlockSpec((1, LANES), lambda i: (0, 0)),
                  pl.BlockSpec((TM, LANES), lambda i: (cfg["rope_blk"](i), 0)),
                  pl.BlockSpec((TM, LANES), lambda i: (cfg["rope_blk"](i), 0))],
        out_specs=[pl.BlockSpec((TM, AW), row), pl.BlockSpec((TM, KW), row),
                   pl.BlockSpec((TM, KW), row), pl.BlockSpec((TM, KW), row),
                   pl.BlockSpec((TM, KW), row)],
        out_shape=[jax.ShapeDtypeStruct((T, AW), BF16), jax.ShapeDtypeStruct((T, KW), BF16),
                   jax.ShapeDtypeStruct((T, KW), BF16), jax.ShapeDtypeStruct((T, KW), F32),
                   jax.ShapeDtypeStruct((T, KW), F32)],
        compiler_params=_cp(("arbitrary",)),
        name="qkv_proj",
    )(h, w, gq, gk, cos_tab, sin_tab)


def _pf_kernel(h_ref, w_ref, cs_ref, up_ref, a_ref, b_ref, *, PW, NG):
    acc = _dot(h_ref[...], w_ref[...])
    up_ref[...] = acc[:, :PW]
    for g in range(NG):
        u = acc[:, PW + g * LANES:PW + (g + 1) * LANES].astype(BF16)
        ab = _dot(u, cs_ref[...])
        a_ref[:, g * LANES:(g + 1) * LANES] = ab[:, :LANES].astype(BF16)
        b_ref[:, g * LANES:(g + 1) * LANES] = ab[:, LANES:].astype(BF16)


def _pool_fourier_proj(h, w, cs, cfg):
    T, D = h.shape
    TM, PW, FW = cfg["TM"], cfg["PW"], cfg["FW"]
    row = lambda i: (i, 0)
    return pl.pallas_call(
        functools.partial(_pf_kernel, PW=PW, NG=FW // LANES),
        grid=(T // TM,),
        in_specs=[pl.BlockSpec((TM, D), row),
                  _resident((D, PW + FW), lambda i: (0, 0)),
                  pl.BlockSpec((LANES, 2 * LANES), lambda i: (0, 0))],
        out_specs=[pl.BlockSpec((TM, PW), row), pl.BlockSpec((TM, FW), row),
                   pl.BlockSpec((TM, FW), row)],
        out_shape=[jax.ShapeDtypeStruct((T, PW), F32), jax.ShapeDtypeStruct((T, FW), BF16),
                   jax.ShapeDtypeStruct((T, FW), BF16)],
        compiler_params=_cp(("arbitrary",)),
        name="pool_fourier_proj",
    )(h, w, cs)


def _gates_kernel(h_ref, w_ref, o_ref):
    z = _dot(h_ref[...], w_ref[...])
    o_ref[...] = (1.0 / (1.0 + jnp.exp(-z))).astype(BF16)


def _gates(h, w, cfg):
    T, D = h.shape
    N = w.shape[1]
    TM = cfg["TM"]
    tn = 1024
    return pl.pallas_call(
        _gates_kernel,
        grid=(N // tn, T // TM),
        in_specs=[pl.BlockSpec((TM, D), lambda j, i: (i, 0)),
                  pl.BlockSpec((D, tn), lambda j, i: (0, j))],
        out_specs=pl.BlockSpec((TM, tn), lambda j, i: (i, j)),
        out_shape=jax.ShapeDtypeStruct((T, N), BF16),
        compiler_params=_cp(("arbitrary", "arbitrary")),
        name="gates_proj",
    )(h, w)


def _attn_kernel(q_ref, k_ref, v_ref, o_ref, m_sc, l_sc, acc_sc, *, G, tq):
    ki = pl.program_id(3)

    @pl.when(ki == 0)
    def _():
        m_sc[...] = jnp.full(m_sc.shape, -jnp.inf, F32)
        l_sc[...] = jnp.zeros(l_sc.shape, F32)
        acc_sc[...] = jnp.zeros(acc_sc.shape, F32)

    q = jnp.concatenate([q_ref[:, g * LANES:(g + 1) * LANES] for g in range(G)], axis=0)
    s = lax.dot_general(q, k_ref[...], (((1,), (1,)), ((), ())), preferred_element_type=F32)
    m_prev = m_sc[...]
    m_new = jnp.maximum(m_prev, jnp.max(s, axis=-1, keepdims=True))
    alpha = jnp.exp(m_prev - m_new)
    p = jnp.exp(s - m_new)
    l_sc[...] = alpha * l_sc[...] + jnp.sum(p, axis=-1, keepdims=True)
    acc_sc[...] = alpha * acc_sc[...] + _dot(p.astype(BF16), v_ref[...])
    m_sc[...] = m_new

    @pl.when(ki == pl.num_programs(3) - 1)
    def _():
        o = acc_sc[...] / l_sc[...]
        for g in range(G):
            o_ref[:, g * LANES:(g + 1) * LANES] = o[g * tq:(g + 1) * tq].astype(BF16)


def _attention(q, k, v, *, q_row0, B, S, Tk, KV, G):
    tq = min(256, S)
    tk = next(t for t in (512, 256, 128) if Tk % t == 0)
    nq, nk = S // tq, Tk // tk
    qb0 = q_row0 // tq
    return pl.pallas_call(
        functools.partial(_attn_kernel, G=G, tq=tq),
        grid=(B, KV, nq, nk),
        in_specs=[pl.BlockSpec((tq, G * LANES), lambda b, h, qi, ki: (qb0 + b * nq + qi, h)),
                  pl.BlockSpec((tk, LANES), lambda b, h, qi, ki: (b * nk + ki, h)),
                  pl.BlockSpec((tk, LANES), lambda b, h, qi, ki: (b * nk + ki, h))],
        out_specs=pl.BlockSpec((tq, G * LANES), lambda b, h, qi, ki: (b * nq + qi, h)),
        out_shape=jax.ShapeDtypeStruct((B * S, KV * G * LANES), BF16),
        scratch_shapes=[pltpu.VMEM((G * tq, 1), F32), pltpu.VMEM((G * tq, 1), F32),
                        pltpu.VMEM((G * tq, LANES), F32)],
        compiler_params=_cp(("arbitrary",) * 4),
        name="attention",
    )(q, k, v)


def _pool_kernel(u_ref, wmap_ref, scale_ref, o_ref, *, R, n_prompt_blocks, Sp, Ss):
    blk = pl.program_id(0)
    g = pl.program_id(1)
    S = jnp.where(blk < n_prompt_blocks, Sp, Ss)
    half = jnp.left_shift(1, g)
    pos = lax.broadcasted_iota(I32, (R, LANES), 0) & (S - 1)
    lo = jnp.maximum(pos - half, 0)
    hi = jnp.minimum(pos + half, S)
    u = u_ref[...]
    tot = jnp.zeros((R, LANES), F32)
    for k in range(-POOL_HALF_MAX, POOL_HALF_MAX):
        valid = jnp.logical_and(pos + k >= lo, pos + k < hi)
        shifted = u if k == 0 else pltpu.roll(u, (-k) % R, axis=0)
        tot = tot + jnp.where(valid, shifted, 0.0)
    cnt = (hi - lo).astype(F32)
    pooled = tot / cnt - u
    mixed = _dot(pooled.astype(BF16), wmap_ref[0]) * scale_ref[...]
    o_ref[...] = mixed.astype(BF16)


def _pool(u, wmap, scale, cfg):
    T, PW = u.shape
    R, NP, Sp, Ss = cfg["Ss"], cfg["NP"], cfg["Sp"], cfg["Ss"]
    return pl.pallas_call(
        functools.partial(_pool_kernel, R=R, n_prompt_blocks=NP // R, Sp=Sp, Ss=Ss),
        grid=(T // R, PW // LANES),
        in_specs=[pl.BlockSpec((R, LANES), lambda b, g: (b, g)),
                  pl.BlockSpec((1, LANES, LANES), lambda b, g: (g, 0, 0)),
                  pl.BlockSpec((1, LANES), lambda b, g: (0, g))],
        out_specs=pl.BlockSpec((R, LANES), lambda b, g: (b, g)),
        out_shape=jax.ShapeDtypeStruct((T, PW), BF16),
        compiler_params=_cp(("arbitrary", "arbitrary")),
        name="multiscale_pool",
    )(u, wmap, scale)


def _fourier_kernel(cs_ref, ss_ref, a_ref, b_ref, o_ref, acc_sc, *, scale):
    k = pl.program_id(2)

    @pl.when(k == 0)
    def _():
        acc_sc[...] = jnp.zeros(acc_sc.shape, F32)

    acc_sc[...] += _dot(cs_ref[...], a_ref[...]) - _dot(ss_ref[...], b_ref[...])

    @pl.when(k == pl.num_programs(2) - 1)
    def _():
        o_ref[...] = (acc_sc[...] * scale).astype(BF16)


def _fourier(a, b, cs, ss, *, row0, B, S):
    FW = a.shape[1]
    tm = min(1024, S)
    tk = min(1024, S)
    nm, nk = S // tm, S // tk
    kb0 = row0 // tk
    return pl.pallas_call(
        functools.partial(_fourier_kernel, scale=float(S * LANES) ** -0.5),
        grid=(B, nm, nk),
        in_specs=[pl.BlockSpec((tm, tk), lambda bb, i, k: (i, k)),
                  pl.BlockSpec((tm, tk), lambda bb, i, k: (i, k)),
                  pl.BlockSpec((tk, FW), lambda bb, i, k: (kb0 + bb * nk + k, 0)),
                  pl.BlockSpec((tk, FW), lambda bb, i, k: (kb0 + bb * nk + k, 0))],
        out_specs=pl.BlockSpec((tm, FW), lambda bb, i, k: (bb * nm + i, 0)),
        out_shape=jax.ShapeDtypeStruct((B * S, FW), BF16),
        scratch_shapes=[pltpu.VMEM((tm, FW), F32)],
        compiler_params=_cp(("arbitrary",) * 3),
        name="fourier_mix",
    )(cs, ss, a, b)


def _merge_kernel(attn_ref, pool_ref, four_ref, gates_ref, x_ref, mod_ref, wa_ref, wp_ref, wf_ref,
                  wo_ref, gpost_ref, gpre_ref, x1_ref, h2_ref, *, D):
    mod = mod_ref[0]
    merged = (gates_ref[:, 0:D].astype(F32) * _dot(attn_ref[...], wa_ref[...])
              + gates_ref[:, D:2 * D].astype(F32) * _dot(pool_ref[...], wp_ref[...])
              + gates_ref[:, 2 * D:3 * D].astype(F32) * _dot(four_ref[...], wf_ref[...]))
    y = _dot(merged.astype(BF16), wo_ref[...])
    x1 = x_ref[...] + mod[:, 2 * D:3 * D] * _rms(y, gpost_ref[...])
    x1_ref[...] = x1
    h2 = _rms(x1, gpre_ref[...]) * (1.0 + mod[:, 4 * D:5 * D]) + mod[:, 3 * D:4 * D]
    h2_ref[...] = h2.astype(h2_ref.dtype)


def _merge(attn, pool, four, gates, x, mod, wa, wp, wf, wo, gpost, gpre, h2_dtype, cfg):
    T, D = x.shape
    TM = cfg["TM_MERGE"]
    scale = cfg["TM"] // TM
    row = lambda i: (i, 0)
    const = lambda i: (0, 0)
    return pl.pallas_call(
        functools.partial(_merge_kernel, D=D),
        grid=(T // TM,),
        in_specs=[pl.BlockSpec((TM, attn.shape[1]), row), pl.BlockSpec((TM, pool.shape[1]), row),
                  pl.BlockSpec((TM, four.shape[1]), row), pl.BlockSpec((TM, 3 * D), row),
                  pl.BlockSpec((TM, D), row),
                  pl.BlockSpec((1, 1, 6 * D), lambda i: (cfg["mod_row"](i // scale), 0, 0)),
                  _resident(wa.shape, const), _resident(wp.shape, const),
                  _resident(wf.shape, const), _resident(wo.shape, const),
                  pl.BlockSpec((1, D), const), pl.BlockSpec((1, D), const)],
        out_specs=[pl.BlockSpec((TM, D), row), pl.BlockSpec((TM, D), row)],
        out_shape=[jax.ShapeDtypeStruct((T, D), F32), jax.ShapeDtypeStruct((T, D), h2_dtype)],
        compiler_params=_cp(("arbitrary",)),
        name="merge_out_proj",
    )(attn, pool, four, gates, x, mod, wa, wp, wf, wo, gpost, gpre)


def _ffn_epilogue(y, x1_ref, mod_ref, gpost_ref, gnext_ref, modn_ref, x2_ref, hn_ref, D):
    mod = mod_ref[0]
    x2 = x1_ref[...] + mod[:, 5 * D:6 * D] * _rms(y, gpost_ref[...])
    x2_ref[...] = x2
    modn = modn_ref[0]
    hn = _rms(x2, gnext_ref[...]) * (1.0 + modn[:, D:2 * D]) + modn[:, 0:D]
    hn_ref[...] = hn.astype(BF16)


def _ffn_kernel(h_ref, wg_ref, wu_ref, wd_ref, x1_ref, mod_ref, gpost_ref, gnext_ref, modn_ref,
                x2_ref, hn_ref, acc_sc, *, D):
    f = pl.program_id(1)

    @pl.when(f == 0)
    def _():
        acc_sc[...] = jnp.zeros(acc_sc.shape, F32)

    h = h_ref[...]
    mid = _silu(_dot(h, wg_ref[...])) * _dot(h, wu_ref[...])
    acc_sc[...] += _dot(mid.astype(BF16), wd_ref[...])

    @pl.when(f == pl.num_programs(1) - 1)
    def _():
        _ffn_epilogue(acc_sc[...], x1_ref, mod_ref, gpost_ref, gnext_ref, modn_ref, x2_ref, hn_ref, D)


def _dense_ffn(h, wg, wu, wd, x1, mod, gpost, gnext, modn, cfg):
    T, D = h.shape
    DFF = wg.shape[1]
    TM = cfg["TM"]
    tf = cfg["TF"]
    row = lambda i, f: (i, 0)
    const = lambda i, f: (0, 0)
    modrow = lambda i, f: (cfg["mod_row"](i), 0, 0)
    return pl.pallas_call(
        functools.partial(_ffn_kernel, D=D),
        grid=(T // TM, DFF // tf),
        in_specs=[pl.BlockSpec((TM, D), row),
                  pl.BlockSpec((D, tf), lambda i, f: (0, f)),
                  pl.BlockSpec((D, tf), lambda i, f: (0, f)),
                  pl.BlockSpec((tf, D), lambda i, f: (f, 0)),
                  pl.BlockSpec((TM, D), row),
                  pl.BlockSpec((1, 1, 6 * D), modrow),
                  pl.BlockSpec((1, D), const), pl.BlockSpec((1, D), const),
                  pl.BlockSpec((1, 1, 6 * D), modrow)],
        out_specs=[pl.BlockSpec((TM, D), row), pl.BlockSpec((TM, D), row)],
        out_shape=[jax.ShapeDtypeStruct((T, D), F32), jax.ShapeDtypeStruct((T, D), BF16)],
        scratch_shapes=[pltpu.VMEM((TM, D), F32)],
        compiler_params=_cp(("arbitrary", "arbitrary")),
        name="dense_ffn",
    )(h, wg, wu, wd, x1, mod, gpost, gnext, modn)


def _router_kernel(h_ref, wr_ref, eidx_ref, rank_ref, wts_ref, cnt_ref, base_sc, *, E, TMR):
    i = pl.program_id(0)

    @pl.when(i == 0)
    def _():
        base_sc[...] = jnp.zeros(base_sc.shape, F32)

    logits = lax.dot_general(wr_ref[...], h_ref[...].astype(F32), (((1,), (1,)), ((), ())),
                             preferred_element_type=F32, precision=lax.Precision.HIGHEST)
    e_iota = lax.broadcasted_iota(I32, (E, TMR), 0)
    m1 = jnp.max(logits, axis=0, keepdims=True)
    i1 = jnp.min(jnp.where(logits == m1, e_iota, E), axis=0, keepdims=True)
    rest = jnp.where(e_iota == i1, -jnp.inf, logits)
    m2 = jnp.max(rest, axis=0, keepdims=True)
    i2 = jnp.min(jnp.where(rest == m2, e_iota, E), axis=0, keepdims=True)
    ex = jnp.exp(m2 - m1)
    w1 = 1.0 / (1.0 + ex)
    w2 = ex / (1.0 + ex)

    oh1 = e_iota == i1
    oh2 = e_iota == i2
    oh = jnp.where(jnp.logical_or(oh1, oh2), 1.0, 0.0)
    tri = jnp.where(lax.broadcasted_iota(I32, (TMR, TMR), 0) < lax.broadcasted_iota(I32, (TMR, TMR), 1),
                    1.0, 0.0).astype(BF16)
    before = _dot(oh.astype(BF16), tri) + base_sc[...]
    r1 = jnp.sum(jnp.where(oh1, before, 0.0), axis=0, keepdims=True)
    r2 = jnp.sum(jnp.where(oh2, before, 0.0), axis=0, keepdims=True)
    base_sc[...] = base_sc[...] + jnp.sum(oh, axis=1, keepdims=True)

    eidx_ref[0:1, :] = i1
    eidx_ref[1:2, :] = i2
    rank_ref[0:1, :] = r1.astype(I32)
    rank_ref[1:2, :] = r2.astype(I32)
    wts_ref[0:1, :] = w1
    wts_ref[1:2, :] = w2
    cnt_ref[...] = jnp.broadcast_to(base_sc[...], cnt_ref.shape).astype(I32)


def _router(h, wr_t, cfg):
    T, D = h.shape
    E = wr_t.shape[0]
    TMR = cfg["TM"]
    col = lambda i: (0, i)
    return pl.pallas_call(
        functools.partial(_router_kernel, E=E, TMR=TMR),
        grid=(T // TMR,),
        in_specs=[pl.BlockSpec((TMR, D), lambda i: (i, 0)),
                  pl.BlockSpec((E, D), lambda i: (0, 0))],
        out_specs=[pl.BlockSpec((TOP_K, TMR), col), pl.BlockSpec((TOP_K, TMR), col),
                   pl.BlockSpec((TOP_K, TMR), col), pl.BlockSpec((E, LANES), lambda i: (0, 0))],
        out_shape=[jax.ShapeDtypeStruct((TOP_K, T), I32), jax.ShapeDtypeStruct((TOP_K, T), I32),
                   jax.ShapeDtypeStruct((TOP_K, T), F32), jax.ShapeDtypeStruct((E, LANES), I32)],
        scratch_shapes=[pltpu.VMEM((E, 1), F32)],
        compiler_params=_cp(("arbitrary",)),
        name="moe_router",
    )(h, wr_t)


def _dispatch_kernel(dest_ref, h_ref, xs_in_ref, xs_ref, sem, *, T, TMD):
    del xs_in_ref
    base = pl.program_id(0) * TMD

    def row_copy(r, k):
        t = base + r
        return pltpu.make_async_copy(h_ref.at[pl.ds(t, 1)],
                                     xs_ref.at[pl.ds(dest_ref[k * T + t], 1)], sem)

    def start(r, c):
        for k in range(TOP_K):
            row_copy(r, k).start()
        return c

    def wait(r, c):
        for k in range(TOP_K):
            row_copy(r, k).wait()
        return c

    lax.fori_loop(0, TMD, start, 0)
    lax.fori_loop(0, TMD, wait, 0)


def _dispatch(dest_flat, h, P, cfg):
    T, D = h.shape
    TMD = cfg["TM"]
    xs0 = jnp.zeros((P, D), h.dtype)
    return pl.pallas_call(
        functools.partial(_dispatch_kernel, T=T, TMD=TMD),
        grid_spec=pltpu.PrefetchScalarGridSpec(
            num_scalar_prefetch=1,
            grid=(T // TMD,),
            in_specs=[pl.BlockSpec(memory_space=pl.ANY), pl.BlockSpec(memory_space=pl.ANY)],
            out_specs=pl.BlockSpec(memory_space=pl.ANY),
            scratch_shapes=[pltpu.SemaphoreType.DMA(())]),
        out_shape=jax.ShapeDtypeStruct((P, D), h.dtype),
        input_output_aliases={2: 0},
        compiler_params=pltpu.CompilerParams(dimension_semantics=("arbitrary",),
                                             has_side_effects=True),
        name="moe_dispatch",
    )(dest_flat, h, xs0)


def _moe_kernel(te_ref, nu_ref, xs_ref, wg_ref, wu_ref, wd_ref, y_ref, acc_sc, xb_sc):
    j = pl.program_id(0)
    f = pl.program_id(1)
    used = j < nu_ref[0]
    last = f == pl.num_programs(1) - 1

    @pl.when(jnp.logical_and(used, f == 0))
    def _():
        acc_sc[...] = jnp.zeros(acc_sc.shape, F32)
        xb_sc[...] = xs_ref[...].astype(BF16)

    @pl.when(used)
    def _():
        x = xb_sc[...]
        mid = _silu(_dot(x, wg_ref[0])) * _dot(x, wu_ref[0])
        acc_sc[...] += _dot(mid.astype(BF16), wd_ref[0])

    @pl.when(jnp.logical_and(used, last))
    def _():
        y_ref[...] = acc_sc[...]

    @pl.when(jnp.logical_and(jnp.logical_not(used), last))
    def _():
        y_ref[...] = jnp.zeros(y_ref.shape, F32)


def _moe_ffn(tile_expert, n_used, xs, wg, wu, wd, cfg):
    P, D = xs.shape
    E, _, DFE = wg.shape
    TME, tf = cfg["TME"], cfg["TF"]
    nf = DFE // tf

    def jj(j, nu):
        return jnp.minimum(j, nu[0] - 1)

    def ff(j, f, nu):
        return jnp.where(j < nu[0], f, nf - 1)

    return pl.pallas_call(
        _moe_kernel,
        grid_spec=pltpu.PrefetchScalarGridSpec(
            num_scalar_prefetch=2,
            grid=(P // TME, nf),
            in_specs=[pl.BlockSpec((TME, D), lambda j, f, te, nu: (jj(j, nu), 0)),
                      pl.BlockSpec((1, D, tf), lambda j, f, te, nu: (te[jj(j, nu)], 0, ff(j, f, nu))),
                      pl.BlockSpec((1, D, tf), lambda j, f, te, nu: (te[jj(j, nu)], 0, ff(j, f, nu))),
                      pl.BlockSpec((1, tf, D), lambda j, f, te, nu: (te[jj(j, nu)], ff(j, f, nu), 0))],
            out_specs=pl.BlockSpec((TME, D), lambda j, f, te, nu: (j, 0)),
            scratch_shapes=[pltpu.VMEM((TME, D), F32), pltpu.VMEM((TME, D), BF16)]),
        out_shape=jax.ShapeDtypeStruct((P, D), F32),
        compiler_params=_cp(("arbitrary", "arbitrary")),
        name="moe_expert_ffn",
    )(tile_expert, n_used, xs, wg, wu, wd)


def _combine_kernel(dest_ref, y_ref, wts_ref, x1_ref, mod_ref, gpost_ref, gnext_ref, modn_ref,
                    x2_ref, hn_ref, ybuf, sem, *, T, TMC, D):
    base = pl.program_id(0) * TMC

    def row_copy(r, k):
        return pltpu.make_async_copy(y_ref.at[pl.ds(dest_ref[k * T + base + r], 1)],
                                     ybuf.at[k, pl.ds(r, 1)], sem)

    def start(r, c):
        for k in range(TOP_K):
            row_copy(r, k).start()
        return c

    def wait(r, c):
        for k in range(TOP_K):
            row_copy(r, k).wait()
        return c

    lax.fori_loop(0, TMC, start, 0)
    lax.fori_loop(0, TMC, wait, 0)
    w = wts_ref[...]
    y = w[:, 0:1] * ybuf[0] + w[:, 1:2] * ybuf[1]
    _ffn_epilogue(y, x1_ref, mod_ref, gpost_ref, gnext_ref, modn_ref, x2_ref, hn_ref, D)


def _combine(dest_flat, y, wts, x1, mod, gpost, gnext, modn, cfg):
    T, D = x1.shape
    TMC = cfg["TM_MERGE"]
    scale = cfg["TM"] // TMC
    row = lambda i, d: (i, 0)
    const = lambda i, d: (0, 0)
    modrow = lambda i, d: (cfg["mod_row"](i // scale), 0, 0)
    return pl.pallas_call(
        functools.partial(_combine_kernel, T=T, TMC=TMC, D=D),
        grid_spec=pltpu.PrefetchScalarGridSpec(
            num_scalar_prefetch=1,
            grid=(T // TMC,),
            in_specs=[pl.BlockSpec(memory_space=pl.ANY),
                      pl.BlockSpec((TMC, TOP_K), row),
                      pl.BlockSpec((TMC, D), row),
                      pl.BlockSpec((1, 1, 6 * D), modrow),
                      pl.BlockSpec((1, D), const), pl.BlockSpec((1, D), const),
                      pl.BlockSpec((1, 1, 6 * D), modrow)],
            out_specs=[pl.BlockSpec((TMC, D), row), pl.BlockSpec((TMC, D), row)],
            scratch_shapes=[pltpu.VMEM((TOP_K, TMC, D), F32), pltpu.SemaphoreType.DMA(())]),
        out_shape=[jax.ShapeDtypeStruct((T, D), F32), jax.ShapeDtypeStruct((T, D), BF16)],
        compiler_params=_cp(("arbitrary",)),
        name="moe_combine",
    )(dest_flat, y, wts, x1, mod, gpost, gnext, modn)


def _rope_tables(Ss, TM):
    n_freq = LANES // 4
    t = jnp.arange(Ss, dtype=I32)
    rows = (t // GRID_W).astype(F32)
    cols = (t % GRID_W).astype(F32)
    inv_freq = jnp.power(ROPE_THETA, -jnp.arange(n_freq, dtype=F32) / n_freq)
    ar = rows[:, None] * inv_freq
    ac = cols[:, None] * inv_freq
    cos = jnp.concatenate([jnp.cos(ar), jnp.cos(ar), jnp.cos(ac), jnp.cos(ac)], axis=-1)
    sin = jnp.concatenate([-jnp.sin(ar), jnp.sin(ar), -jnp.sin(ac), jnp.sin(ac)], axis=-1)
    cos = jnp.concatenate([jnp.ones((TM, LANES), F32), cos], axis=0)
    sin = jnp.concatenate([jnp.zeros((TM, LANES), F32), sin], axis=0)
    return cos, sin


def _dft_tables(n):
    j = jnp.arange(n, dtype=I32)
    ph = (j[:, None] * j[None, :]) % n
    ang = ph.astype(F32) * (2.0 * math.pi / n)
    return jnp.cos(ang).astype(BF16), jnp.sin(ang).astype(BF16)


def kernel(x_prompt, x_sample, cache_k, cache_v, c, c_ctx, w_mod, b_mod, g_pre_mix, g_post_mix,
           g_pre_ffn, g_post_ffn, w_in, g_q, g_k, w_pool_map, pool_scale, w_attn_o, w_pool_o,
           w_four_o, w_out, w_ffn_gate, w_ffn_up, w_ffn_down, w_router, w_exp_gate, w_exp_up,
           w_exp_down):
    Bp, Sp, D = x_prompt.shape
    Bs, Ss, _ = x_sample.shape
    L = w_mod.shape[0]
    PAST, KV = cache_k.shape[2], cache_k.shape[3]
    AW, PW, FW = w_attn_o.shape[1], w_pool_o.shape[1], w_four_o.shape[1]
    NH = AW // LANES
    G = NH // KV
    KW = KV * LANES
    E = w_router.shape[2]
    NP, NS = Bp * Sp, Bs * Ss
    T = NP + NS
    TM = min(512, math.gcd(NP, Ss))
    assert cache_k.shape[4] == LANES and NP % TM == 0 and Ss % TM == 0 and NP % Ss == 0
    assert Ss % Sp == 0 and Sp & (Sp - 1) == 0 and Ss & (Ss - 1) == 0 and Ss % GRID_W == 0
    NPT, TPS = NP // TM, Ss // TM
    TME = TM
    P = (-(-(T * TOP_K) // TME) + E) * TME

    cfg = dict(
        TM=TM, TM_MERGE=min(TM, 256), TME=TME, TF=512 if w_ffn_gate.shape[2] % 512 == 0 else 256,
        NH=NH, KV=KV, PW=PW, FW=FW, NP=NP, Sp=Sp, Ss=Ss,
        mod_row=lambda i: jnp.where(i < NPT, 0, 1 + (i - NPT) // TPS),
        rope_blk=lambda i: jnp.where(i < NPT, 0, 1 + (i - NPT) % TPS),
    )

    x = jnp.concatenate([x_prompt.reshape(NP, D), x_sample.reshape(NS, D)], axis=0)
    R = -(-(1 + Bs) // 8) * 8
    cond = jnp.zeros((R, D), F32).at[0].set(c_ctx).at[1:1 + Bs].set(c)
    mod_all = _modulation(cond, w_mod, b_mod).reshape(L, R, 1, 6 * D)

    o1, o2, o3, o4 = AW, AW + 2 * KW, AW + 2 * KW + PW, AW + 2 * KW + PW + FW
    w_in_b = w_in.astype(BF16)
    cos_tab, sin_tab = _rope_tables(Ss, TM)
    cc, sc = _dft_tables(LANES)
    cs_chan = jnp.concatenate([cc, sc], axis=1)
    dft_p = _dft_tables(Sp)
    dft_s = _dft_tables(Ss)
    ck = cache_k.astype(BF16).reshape(Bs, L, PAST, KW)
    cv = cache_v.astype(BF16).reshape(Bs, L, PAST, KW)
    row1 = lambda a: a.reshape(1, -1)

    h = _norm_mod(x, row1(g_pre_mix[0]), mod_all[0], cfg)
    new_k, new_v = [], []
    for l in range(L):
        mod = mod_all[l]
        q, kb, vb, kf, vf = _qkv(h, w_in_b[l, :, :o2], row1(g_q[l]), row1(g_k[l]), cos_tab, sin_tab, cfg)
        u_pool, fa, fb = _pool_fourier_proj(h, w_in_b[l, :, o2:o4], cs_chan, cfg)
        gates = _gates(h, w_in_b[l, :, o4:], cfg)
        new_k.append(kf[:NP].reshape(Bp, Sp, KV, LANES))
        new_v.append(vf[:NP].reshape(Bp, Sp, KV, LANES))

        attn_p = _attention(q, kb, vb, q_row0=0, B=Bp, S=Sp, Tk=Sp, KV=KV, G=G)
        k_s = jnp.concatenate([ck[:, l], kb[NP:].reshape(Bs, Ss, KW)], axis=1).reshape(-1, KW)
        v_s = jnp.concatenate([cv[:, l], vb[NP:].reshape(Bs, Ss, KW)], axis=1).reshape(-1, KW)
        attn_s = _attention(q, k_s, v_s, q_row0=NP, B=Bs, S=Ss, Tk=PAST + Ss, KV=KV, G=G)
        attn = jnp.concatenate([attn_p, attn_s], axis=0)

        pool = _pool(u_pool, w_pool_map[l].astype(BF16), row1(pool_scale[l]), cfg)
        four = jnp.concatenate([
            _fourier(fa, fb, *dft_p, row0=0, B=Bp, S=Sp),
            _fourier(fa, fb, *dft_s, row0=NP, B=Bs, S=Ss)], axis=0)

        x1, h2 = _merge(attn, pool, four, gates, x, mod, w_attn_o[l].astype(BF16),
                        w_pool_o[l].astype(BF16), w_four_o[l].astype(BF16), w_out[l].astype(BF16),
                        row1(g_post_mix[l]), row1(g_pre_ffn[l]), BF16 if l % 2 == 0 else F32, cfg)

        ln = min(l + 1, L - 1)
        gnext, modn = row1(g_pre_mix[ln]), mod_all[ln]
        i = l // 2
        if l % 2 == 0:
            x, h = _dense_ffn(h2, w_ffn_gate[i].astype(BF16), w_ffn_up[i].astype(BF16),
                              w_ffn_down[i].astype(BF16), x1, mod, row1(g_post_ffn[l]), gnext, modn, cfg)
        else:
            eidx, rank, wts, cnt = _router(h2, w_router[i].T, cfg)
            counts = cnt[:, 0]
            padded = (counts + TME - 1) // TME * TME
            pad_end = jnp.cumsum(padded)
            pad_start = pad_end - padded
            dest = (jnp.take(pad_start, eidx) + rank).reshape(-1).astype(I32)
            n_used = (pad_end[-1] // TME).astype(I32).reshape(1)
            tile_start = jnp.arange(P // TME, dtype=I32) * TME
            tile_expert = jnp.minimum(
                jnp.sum(tile_start[:, None] >= pad_end[None, :], axis=1), E - 1).astype(I32)
            xs = _dispatch(dest, h2, P, cfg)
            y = _moe_ffn(tile_expert, n_used, xs, w_exp_gate[i].astype(BF16),
                         w_exp_up[i].astype(BF16), w_exp_down[i].astype(BF16), cfg)
            x, h = _combine(dest, y, wts.T, x1, mod, row1(g_post_ffn[l]), gnext, modn, cfg)

    y_prompt = x[:NP].reshape(Bp, Sp, D)
    y_sample = x[NP:].reshape(Bs, Ss, D)
    return (y_prompt, y_sample, jnp.stack(new_k, axis=1), jnp.stack(new_v, axis=1))
```

```python
import functools
import math

import jax
import jax.numpy as jnp
from jax import lax
from jax.experimental import pallas as pl
from jax.experimental.pallas import tpu as pltpu

F32 = jnp.float32
BF16 = jnp.bfloat16
I32 = jnp.int32

LANES = 128
GRID_W = 64
ROPE_THETA = 10000.0
EPS = 1e-6
TOP_K = 2
POOL_WINDOWS = (2, 4, 8, 16)
POOL_HALF_MAX = max(POOL_WINDOWS) // 2
VMEM_LIMIT = 56 * 1024 * 1024


def _cp(sem, vmem=VMEM_LIMIT):
    return pltpu.CompilerParams(dimension_semantics=sem, vmem_limit_bytes=vmem)


def _rms(x, g):
    return x * lax.rsqrt(jnp.mean(x * x, axis=-1, keepdims=True) + EPS) * g


def _silu(x):
    return x / (1.0 + jnp.exp(-x))


def _dot(a, b):
    return jnp.dot(a, b, preferred_element_type=F32)


def _resident(shape, index_map):
    return pl.BlockSpec(shape, index_map, pipeline_mode=pl.Buffered(1))


def _mod_kernel(c_ref, w_ref, b_ref, o_ref):
    s = _silu(c_ref[...])
    o_ref[0] = jnp.dot(s, w_ref[0], preferred_element_type=F32,
                       precision=lax.Precision.HIGHEST) + b_ref[0]


def _modulation(cond, w_mod, b_mod):
    L, D, N = w_mod.shape
    R = cond.shape[0]
    tn = 1024
    return pl.pallas_call(
        _mod_kernel,
        grid=(L, N // tn),
        in_specs=[pl.BlockSpec((R, D), lambda l, j: (0, 0)),
                  pl.BlockSpec((1, D, tn), lambda l, j: (l, 0, j)),
                  pl.BlockSpec((1, 1, tn), lambda l, j: (l, 0, j))],
        out_specs=pl.BlockSpec((1, R, tn), lambda l, j: (l, 0, j)),
        out_shape=jax.ShapeDtypeStruct((L, R, N), F32),
        compiler_params=_cp(("arbitrary", "arbitrary")),
        name="modulation",
    )(cond, w_mod, b_mod.reshape(L, 1, N))


def _norm_mod_kernel(x_ref, g_ref, mod_ref, h_ref, *, D):
    mod = mod_ref[0]
    y = _rms(x_ref[...], g_ref[...])
    h_ref[...] = (y * (1.0 + mod[:, D:2 * D]) + mod[:, 0:D]).astype(BF16)


def _norm_mod(x, g, mod, cfg):
    T, D = x.shape
    TM = cfg["TM"]
    return pl.pallas_call(
        functools.partial(_norm_mod_kernel, D=D),
        grid=(T // TM,),
        in_specs=[pl.BlockSpec((TM, D), lambda i: (i, 0)),
                  pl.BlockSpec((1, D), lambda i: (0, 0)),
                  pl.BlockSpec((1, 1, 6 * D), lambda i: (cfg["mod_row"](i), 0, 0))],
        out_specs=pl.BlockSpec((TM, D), lambda i: (i, 0)),
        out_shape=jax.ShapeDtypeStruct((T, D), BF16),
        compiler_params=_cp(("arbitrary",)),
        name="norm_mod",
    )(x, g, mod)


def _qkv_kernel(h_ref, w_ref, gq_ref, gk_ref, cos_ref, sin_ref,
                q_ref, kb_ref, vb_ref, kf_ref, vf_ref, *, NH, KV, scale):
    acc = _dot(h_ref[...], w_ref[...])
    cos = cos_ref[...]
    sin = sin_ref[...]
    lane = lax.broadcasted_iota(I32, cos.shape, 1)
    first_half = (lane % (LANES // 2)) < (LANES // 4)

    def rope(y):
        partner = jnp.where(first_half, pltpu.roll(y, LANES - LANES // 4, axis=1),
                            pltpu.roll(y, LANES // 4, axis=1))
        return y * cos + partner * sin

    for hd in range(NH):
        z = acc[:, hd * LANES:(hd + 1) * LANES]
        q_ref[:, hd * LANES:(hd + 1) * LANES] = (rope(_rms(z, gq_ref[...])) * scale).astype(BF16)
    for hd in range(KV):
        z = acc[:, (NH + hd) * LANES:(NH + hd + 1) * LANES]
        y = _rms(z, gk_ref[...])
        kf_ref[:, hd * LANES:(hd + 1) * LANES] = y
        kb_ref[:, hd * LANES:(hd + 1) * LANES] = rope(y).astype(BF16)
    v = acc[:, (NH + KV) * LANES:(NH + 2 * KV) * LANES]
    vf_ref[...] = v
    vb_ref[...] = v.astype(BF16)


def _qkv(h, w, gq, gk, cos_tab, sin_tab, cfg):
    T, D = h.shape
    TM, NH, KV = cfg["TM"], cfg["NH"], cfg["KV"]
    AW, KW = NH * LANES, KV * LANES
    row = lambda i: (i, 0)
    return pl.pallas_call(
        functools.partial(_qkv_kernel, NH=NH, KV=KV, scale=float(LANES) ** -0.5 * math.log2(math.e)),
        grid=(T // TM,),
        in_specs=[pl.BlockSpec((TM, D), row),
                  _resident((D, AW + 2 * KW), lambda i: (0, 0)),
                  pl.BlockSpec((1, LANES), lambda i: (0, 0)),
                  pl.BlockSpec((1, LANES), lambda i: (0, 0)),
                  pl.BlockSpec((TM, LANES), lambda i: (cfg["rope_blk"](i), 0)),
                  pl.BlockSpec((TM, LANES), lambda i: (cfg["rope_blk"](i), 0))],
        out_specs=[pl.BlockSpec((TM, AW), row), pl.BlockSpec((TM, KW), row),
                   pl.BlockSpec((TM, KW), row), pl.BlockSpec((TM, KW), row),
                   pl.BlockSpec((TM, KW), row)],
        out_shape=[jax.ShapeDtypeStruct((T, AW), BF16), jax.ShapeDtypeStruct((T, KW), BF16),
                   jax.ShapeDtypeStruct((T, KW), BF16), jax.ShapeDtypeStruct((T, KW), F32),
                   jax.ShapeDtypeStruct((T, KW), F32)],
        compiler_params=_cp(("arbitrary",)),
        name="qkv_proj",
    )(h, w, gq, gk, cos_tab, sin_tab)


def _pf_kernel(h_ref, w_ref, cs_ref, up_ref, a_ref, b_ref, *, PW, NG):
    acc = _dot(h_ref[...], w_ref[...])
    up_ref[...] = acc[:, :PW]
    for g in range(NG):
        u = acc[:, PW + g * LANES:PW + (g + 1) * LANES].astype(BF16)
        ab = _dot(u, cs_ref[...])
        a_ref[:, g * LANES:(g + 1) * LANES] = ab[:, :LANES].astype(BF16)
        b_ref[:, g * LANES:(g + 1) * LANES] = ab[:, LANES:].astype(BF16)


def _pool_fourier_proj(h, w, cs, cfg):
    T, D = h.shape
    TM, PW, FW = cfg["TM"], cfg["PW"], cfg["FW"]
    row = lambda i: (i, 0)
    return pl.pallas_call(
        functools.partial(_pf_kernel, PW=PW, NG=FW // LANES),
        grid=(T // TM,),
        in_specs=[pl.BlockSpec((TM, D), row),
                  _resident((D, PW + FW), lambda i: (0, 0)),
                  pl.BlockSpec((LANES, 2 * LANES), lambda i: (0, 0))],
        out_specs=[pl.BlockSpec((TM, PW), row), pl.BlockSpec((TM, FW), row),
                   pl.BlockSpec((TM, FW), row)],
        out_shape=[jax.ShapeDtypeStruct((T, PW), F32), jax.ShapeDtypeStruct((T, FW), BF16),
                   jax.ShapeDtypeStruct((T, FW), BF16)],
        compiler_params=_cp(("arbitrary",)),
        name="pool_fourier_proj",
    )(h, w, cs)


def _gates_kernel(h_ref, w_ref, o_ref):
    z = _dot(h_ref[...], w_ref[...])
    o_ref[...] = (1.0 / (1.0 + jnp.exp(-z))).astype(BF16)


def _gates(h, w, cfg):
    T, D = h.shape
    N = w.shape[1]
    TM = cfg["TM"]
    tn = 1024
    return pl.pallas_call(
        _gates_kernel,
        grid=(N // tn, T // TM),
        in_specs=[pl.BlockSpec((TM, D), lambda j, i: (i, 0)),
                  pl.BlockSpec((D, tn), lambda j, i: (0, j))],
        out_specs=pl.BlockSpec((TM, tn), lambda j, i: (i, j)),
        out_shape=jax.ShapeDtypeStruct((T, N), BF16),
        compiler_params=_cp(("arbitrary", "arbitrary")),
        name="gates_proj",
    )(h, w)


def _attn_kernel(q_ref, k_ref, v_ref, o_ref, *, G, tq, tk, nk):
    q = jnp.concatenate([q_ref[:, g * LANES:(g + 1) * LANES] for g in range(G)], axis=0)
    rows = G * tq
    nt = tk // LANES
    m = jnp.full((rows, LANES), -jnp.inf, F32)
    l = jnp.zeros((rows, LANES), F32)
    acc = jnp.zeros((rows, LANES), F32)
    for c in range(nk):
        k = k_ref[c * tk:(c + 1) * tk, :]
        v = v_ref[c * tk:(c + 1) * tk, :]
        s = lax.dot_general(q, k, (((1,), (1,)), ((), ())), preferred_element_type=F32)
        m_new = jnp.maximum(m, jnp.max(s, axis=-1, keepdims=True))
        alpha = jnp.exp2(m - m_new)
        p = [jnp.exp2(s[:, t * LANES:(t + 1) * LANES] - m_new) for t in range(nt)]
        l = alpha * l + functools.reduce(lambda a, b: a + b, p)
        pb = jnp.concatenate([pt.astype(BF16) for pt in p], axis=1)
        acc = alpha * acc + _dot(pb, v)
        m = m_new
    o = acc / jnp.sum(l, axis=-1, keepdims=True)
    for g in range(G):
        o_ref[:, g * LANES:(g + 1) * LANES] = o[g * tq:(g + 1) * tq].astype(BF16)


def _attention(q, k, v, *, q_row0, B, S, Tk, KV, G):
    tq = min(256, S)
    tk = next(t for t in (512, 256, 128) if Tk % t == 0)
    nq, nk = S // tq, Tk // tk
    qb0 = q_row0 // tq
    return pl.pallas_call(
        functools.partial(_attn_kernel, G=G, tq=tq, tk=tk, nk=nk),
        grid=(B, KV, nq),
        in_specs=[pl.BlockSpec((tq, G * LANES), lambda b, h, qi: (qb0 + b * nq + qi, h)),
                  pl.BlockSpec((Tk, LANES), lambda b, h, qi: (b, h)),
                  pl.BlockSpec((Tk, LANES), lambda b, h, qi: (b, h))],
        out_specs=pl.BlockSpec((tq, G * LANES), lambda b, h, qi: (b * nq + qi, h)),
        out_shape=jax.ShapeDtypeStruct((B * S, KV * G * LANES), BF16),
        compiler_params=_cp(("arbitrary",) * 3),
        name="attention",
    )(q, k, v)


def _pool_kernel(u_ref, wmap_ref, scale_ref, o_ref, tot_sc, *, R, n_prompt_blocks, Sp, Ss):
    blk = pl.program_id(0)
    g = pl.program_id(1)
    S = jnp.where(blk < n_prompt_blocks, Sp, Ss)
    half = jnp.left_shift(1, g)
    pos = lax.broadcasted_iota(I32, (R, LANES), 0) & (S - 1)

    def add_offsets(offsets):
        u = u_ref[...]
        tot = tot_sc[...]
        for k in offsets:
            valid = jnp.logical_and(pos + k >= 0, pos + k < S)
            tot = tot + jnp.where(valid, pltpu.roll(u, (-k) % R, axis=0), 0.0)
        tot_sc[...] = tot

    tot_sc[...] = u_ref[...]
    add_offsets([-1])
    for lvl in range(1, len(POOL_WINDOWS)):
        h0, h1 = POOL_WINDOWS[lvl - 1] // 2, POOL_WINDOWS[lvl] // 2

        @pl.when(g >= lvl)
        def _():
            add_offsets(list(range(-h1, -h0)) + list(range(h0, h1)))

    cnt = (jnp.minimum(pos + half, S) - jnp.maximum(pos - half, 0)).astype(F32)
    pooled = tot_sc[...] / cnt - u_ref[...]
    mixed = _dot(pooled.astype(BF16), wmap_ref[0]) * scale_ref[...]
    o_ref[...] = mixed.astype(BF16)


def _pool(u, wmap, scale, cfg):
    T, PW = u.shape
    R, NP, Sp, Ss = cfg["Ss"], cfg["NP"], cfg["Sp"], cfg["Ss"]
    return pl.pallas_call(
        functools.partial(_pool_kernel, R=R, n_prompt_blocks=NP // R, Sp=Sp, Ss=Ss),
        grid=(T // R, PW // LANES),
        in_specs=[pl.BlockSpec((R, LANES), lambda b, g: (b, g)),
                  pl.BlockSpec((1, LANES, LANES), lambda b, g: (g, 0, 0)),
                  pl.BlockSpec((1, LANES), lambda b, g: (0, g))],
        out_specs=pl.BlockSpec((R, LANES), lambda b, g: (b, g)),
        out_shape=jax.ShapeDtypeStruct((T, PW), BF16),
        scratch_shapes=[pltpu.VMEM((R, LANES), F32)],
        compiler_params=_cp(("arbitrary", "arbitrary")),
        name="multiscale_pool",
    )(u, wmap, scale)


def _fourier_kernel(cs_ref, ss_ref, a_ref, b_ref, o_ref, acc_sc, *, scale):
    k = pl.program_id(2)

    @pl.when(k == 0)
    def _():
        acc_sc[...] = jnp.zeros(acc_sc.shape, F32)

    acc_sc[...] += _dot(cs_ref[...], a_ref[...]) - _dot(ss_ref[...], b_ref[...])

    @pl.when(k == pl.num_programs(2) - 1)
    def _():
        o_ref[...] = (acc_sc[...] * scale).astype(BF16)


def _fourier(a, b, cs, ss, *, row0, B, S):
    FW = a.shape[1]
    tm = min(1024, S)
    tk = min(1024, S)
    nm, nk = S // tm, S // tk
    kb0 = row0 // tk
    return pl.pallas_call(
        functools.partial(_fourier_kernel, scale=float(S * LANES) ** -0.5),
        grid=(B, nm, nk),
        in_specs=[pl.BlockSpec((tm, tk), lambda bb, i, k: (i, k)),
                  pl.BlockSpec((tm, tk), lambda bb, i, k: (i, k)),
                  pl.BlockSpec((tk, FW), lambda bb, i, k: (kb0 + bb * nk + k, 0)),
                  pl.BlockSpec((tk, FW), lambda bb, i, k: (kb0 + bb * nk + k, 0))],
        out_specs=pl.BlockSpec((tm, FW), lambda bb, i, k: (bb * nm + i, 0)),
        out_shape=jax.ShapeDtypeStruct((B * S, FW), BF16),
        scratch_shapes=[pltpu.VMEM((tm, FW), F32)],
        compiler_params=_cp(("arbitrary",) * 3),
        name="fourier_mix",
    )(cs, ss, a, b)


def _merge_kernel(attn_ref, pool_ref, four_ref, gates_ref, x_ref, mod_ref, wa_ref, wp_ref, wf_ref,
                  wo_ref, gpost_ref, gpre_ref, x1_ref, h2_ref, *, D):
    mod = mod_ref[0]
    merged = (gates_ref[:, 0:D].astype(F32) * _dot(attn_ref[...], wa_ref[...])
              + gates_ref[:, D:2 * D].astype(F32) * _dot(pool_ref[...], wp_ref[...])
              + gates_ref[:, 2 * D:3 * D].astype(F32) * _dot(four_ref[...], wf_ref[...]))
    y = _dot(merged.astype(BF16), wo_ref[...])
    x1 = x_ref[...] + mod[:, 2 * D:3 * D] * _rms(y, gpost_ref[...])
    x1_ref[...] = x1
    h2 = _rms(x1, gpre_ref[...]) * (1.0 + mod[:, 4 * D:5 * D]) + mod[:, 3 * D:4 * D]
    h2_ref[...] = h2.astype(h2_ref.dtype)


def _merge(attn, pool, four, gates, x, mod, wa, wp, wf, wo, gpost, gpre, h2_dtype, cfg):
    T, D = x.shape
    TM = cfg["TM_MERGE"]
    scale = cfg["TM"] // TM
    row = lambda i: (i, 0)
    const = lambda i: (0, 0)
    return pl.pallas_call(
        functools.partial(_merge_kernel, D=D),
        grid=(T // TM,),
        in_specs=[pl.BlockSpec((TM, attn.shape[1]), row), pl.BlockSpec((TM, pool.shape[1]), row),
                  pl.BlockSpec((TM, four.shape[1]), row), pl.BlockSpec((TM, 3 * D), row),
                  pl.BlockSpec((TM, D), row),
                  pl.BlockSpec((1, 1, 6 * D), lambda i: (cfg["mod_row"](i // scale), 0, 0)),
                  _resident(wa.shape, const), _resident(wp.shape, const),
                  _resident(wf.shape, const), _resident(wo.shape, const),
                  pl.BlockSpec((1, D), const), pl.BlockSpec((1, D), const)],
        out_specs=[pl.BlockSpec((TM, D), row), pl.BlockSpec((TM, D), row)],
        out_shape=[jax.ShapeDtypeStruct((T, D), F32), jax.ShapeDtypeStruct((T, D), h2_dtype)],
        compiler_params=_cp(("arbitrary",)),
        name="merge_out_proj",
    )(attn, pool, four, gates, x, mod, wa, wp, wf, wo, gpost, gpre)


def _ffn_epilogue(y, x1_ref, mod_ref, gpost_ref, gnext_ref, modn_ref, x2_ref, hn_ref, D):
    mod = mod_ref[0]
    x2 = x1_ref[...] + mod[:, 5 * D:6 * D] * _rms(y, gpost_ref[...])
    x2_ref[...] = x2
    modn = modn_ref[0]
    hn = _rms(x2, gnext_ref[...]) * (1.0 + modn[:, D:2 * D]) + modn[:, 0:D]
    hn_ref[...] = hn.astype(BF16)


def _ffn_kernel(h_ref, wg_ref, wu_ref, wd_ref, x1_ref, mod_ref, gpost_ref, gnext_ref, modn_ref,
                x2_ref, hn_ref, acc_sc, *, D):
    f = pl.program_id(1)

    @pl.when(f == 0)
    def _():
        acc_sc[...] = jnp.zeros(acc_sc.shape, F32)

    h = h_ref[...]
    mid = _silu(_dot(h, wg_ref[...])) * _dot(h, wu_ref[...])
    acc_sc[...] += _dot(mid.astype(BF16), wd_ref[...])

    @pl.when(f == pl.num_programs(1) - 1)
    def _():
        _ffn_epilogue(acc_sc[...], x1_ref, mod_ref, gpost_ref, gnext_ref, modn_ref, x2_ref, hn_ref, D)


def _dense_ffn(h, wg, wu, wd, x1, mod, gpost, gnext, modn, cfg):
    T, D = h.shape
    DFF = wg.shape[1]
    TM = cfg["TM"]
    tf = cfg["TF"]
    row = lambda i, f: (i, 0)
    const = lambda i, f: (0, 0)
    modrow = lambda i, f: (cfg["mod_row"](i), 0, 0)
    return pl.pallas_call(
        functools.partial(_ffn_kernel, D=D),
        grid=(T // TM, DFF // tf),
        in_specs=[pl.BlockSpec((TM, D), row),
                  pl.BlockSpec((D, tf), lambda i, f: (0, f)),
                  pl.BlockSpec((D, tf), lambda i, f: (0, f)),
                  pl.BlockSpec((tf, D), lambda i, f: (f, 0)),
                  pl.BlockSpec((TM, D), row),
                  pl.BlockSpec((1, 1, 6 * D), modrow),
                  pl.BlockSpec((1, D), const), pl.BlockSpec((1, D), const),
                  pl.BlockSpec((1, 1, 6 * D), modrow)],
        out_specs=[pl.BlockSpec((TM, D), row), pl.BlockSpec((TM, D), row)],
        out_shape=[jax.ShapeDtypeStruct((T, D), F32), jax.ShapeDtypeStruct((T, D), BF16)],
        scratch_shapes=[pltpu.VMEM((TM, D), F32)],
        compiler_params=_cp(("arbitrary", "arbitrary")),
        name="dense_ffn",
    )(h, wg, wu, wd, x1, mod, gpost, gnext, modn)


def _router_kernel(h_ref, wr_ref, eidx_ref, rank_ref, wts_ref, cnt_ref, base_sc, *, E, TMR):
    i = pl.program_id(0)

    @pl.when(i == 0)
    def _():
        base_sc[...] = jnp.zeros(base_sc.shape, F32)

    logits = lax.dot_general(wr_ref[...], h_ref[...].astype(F32), (((1,), (1,)), ((), ())),
                             preferred_element_type=F32, precision=lax.Precision.HIGHEST)
    e_iota = lax.broadcasted_iota(I32, (E, TMR), 0)
    m1 = jnp.max(logits, axis=0, keepdims=True)
    i1 = jnp.min(jnp.where(logits == m1, e_iota, E), axis=0, keepdims=True)
    rest = jnp.where(e_iota == i1, -jnp.inf, logits)
    m2 = jnp.max(rest, axis=0, keepdims=True)
    i2 = jnp.min(jnp.where(rest == m2, e_iota, E), axis=0, keepdims=True)
    ex = jnp.exp(m2 - m1)
    w1 = 1.0 / (1.0 + ex)
    w2 = ex / (1.0 + ex)

    oh1 = e_iota == i1
    oh2 = e_iota == i2
    oh = jnp.where(jnp.logical_or(oh1, oh2), 1.0, 0.0)
    tri = jnp.where(lax.broadcasted_iota(I32, (TMR, TMR), 0) < lax.broadcasted_iota(I32, (TMR, TMR), 1),
                    1.0, 0.0).astype(BF16)
    before = _dot(oh.astype(BF16), tri) + base_sc[...]
    r1 = jnp.sum(jnp.where(oh1, before, 0.0), axis=0, keepdims=True)
    r2 = jnp.sum(jnp.where(oh2, before, 0.0), axis=0, keepdims=True)
    base_sc[...] = base_sc[...] + jnp.sum(oh, axis=1, keepdims=True)

    eidx_ref[0:1, :] = i1
    eidx_ref[1:2, :] = i2
    rank_ref[0:1, :] = r1.astype(I32)
    rank_ref[1:2, :] = r2.astype(I32)
    wts_ref[0:1, :] = w1
    wts_ref[1:2, :] = w2
    cnt_ref[...] = jnp.broadcast_to(base_sc[...], cnt_ref.shape).astype(I32)


def _router(h, wr_t, cfg):
    T, D = h.shape
    E = wr_t.shape[0]
    TMR = cfg["TM"]
    col = lambda i: (0, i)
    return pl.pallas_call(
        functools.partial(_router_kernel, E=E, TMR=TMR),
        grid=(T // TMR,),
        in_specs=[pl.BlockSpec((TMR, D), lambda i: (i, 0)),
                  pl.BlockSpec((E, D), lambda i: (0, 0))],
        out_specs=[pl.BlockSpec((TOP_K, TMR), col), pl.BlockSpec((TOP_K, TMR), col),
                   pl.BlockSpec((TOP_K, TMR), col), pl.BlockSpec((E, LANES), lambda i: (0, 0))],
        out_shape=[jax.ShapeDtypeStruct((TOP_K, T), I32), jax.ShapeDtypeStruct((TOP_K, T), I32),
                   jax.ShapeDtypeStruct((TOP_K, T), F32), jax.ShapeDtypeStruct((E, LANES), I32)],
        scratch_shapes=[pltpu.VMEM((E, 1), F32)],
        compiler_params=_cp(("arbitrary",)),
        name="moe_router",
    )(h, wr_t)


def _dispatch_kernel(dest_ref, h_ref, xs_in_ref, xs_ref, sem, *, T, TMD):
    del xs_in_ref
    base = pl.program_id(0) * TMD

    def row_copy(r, k):
        return pltpu.make_async_copy(h_ref.at[pl.ds(r, 1)],
                                     xs_ref.at[pl.ds(dest_ref[k * T + base + r], 1)], sem)

    def start(r, c):
        for k in range(TOP_K):
            row_copy(r, k).start()
        return c

    def wait(r, c):
        for k in range(TOP_K):
            row_copy(r, k).wait()
        return c

    lax.fori_loop(0, TMD, start, 0)
    lax.fori_loop(0, TMD, wait, 0)


def _dispatch(dest_flat, h, P, cfg):
    T, D = h.shape
    TMD = cfg["TM"]
    xs0 = jnp.zeros((P, D), h.dtype)
    return pl.pallas_call(
        functools.partial(_dispatch_kernel, T=T, TMD=TMD),
        grid_spec=pltpu.PrefetchScalarGridSpec(
            num_scalar_prefetch=1,
            grid=(T // TMD,),
            in_specs=[pl.BlockSpec((TMD, D), lambda i, dest: (i, 0)),
                      pl.BlockSpec(memory_space=pl.ANY)],
            out_specs=pl.BlockSpec(memory_space=pl.ANY),
            scratch_shapes=[pltpu.SemaphoreType.DMA(())]),
        out_shape=jax.ShapeDtypeStruct((P, D), h.dtype),
        input_output_aliases={2: 0},
        compiler_params=pltpu.CompilerParams(dimension_semantics=("arbitrary",),
                                             has_side_effects=True),
        name="moe_dispatch",
    )(dest_flat, h, xs0)


def _moe_kernel(te_ref, nu_ref, xs_ref, wg_ref, wu_ref, wd_ref, y_ref, acc_sc, xb_sc):
    j = pl.program_id(0)
    f = pl.program_id(1)
    used = j < nu_ref[0]
    last = f == pl.num_programs(1) - 1

    @pl.when(jnp.logical_and(used, f == 0))
    def _():
        acc_sc[...] = jnp.zeros(acc_sc.shape, F32)
        xb_sc[...] = xs_ref[...].astype(BF16)

    @pl.when(used)
    def _():
        x = xb_sc[...]
        mid = _silu(_dot(x, wg_ref[0])) * _dot(x, wu_ref[0])
        acc_sc[...] += _dot(mid.astype(BF16), wd_ref[0])

    @pl.when(jnp.logical_and(used, last))
    def _():
        y_ref[...] = acc_sc[...]

    @pl.when(jnp.logical_and(jnp.logical_not(used), last))
    def _():
        y_ref[...] = jnp.zeros(y_ref.shape, F32)


def _moe_ffn(tile_expert, n_used, xs, wg, wu, wd, cfg):
    P, D = xs.shape
    E, _, DFE = wg.shape
    TME, tf = cfg["TME"], cfg["TF"]
    nf = DFE // tf

    def jj(j, nu):
        return jnp.minimum(j, nu[0] - 1)

    def ff(j, f, nu):
        return jnp.where(j < nu[0], f, nf - 1)

    return pl.pallas_call(
        _moe_kernel,
        grid_spec=pltpu.PrefetchScalarGridSpec(
            num_scalar_prefetch=2,
            grid=(P // TME, nf),
            in_specs=[pl.BlockSpec((TME, D), lambda j, f, te, nu: (jj(j, nu), 0)),
                      pl.BlockSpec((1, D, tf), lambda j, f, te, nu: (te[jj(j, nu)], 0, ff(j, f, nu))),
                      pl.BlockSpec((1, D, tf), lambda j, f, te, nu: (te[jj(j, nu)], 0, ff(j, f, nu))),
                      pl.BlockSpec((1, tf, D), lambda j, f, te, nu: (te[jj(j, nu)], ff(j, f, nu), 0))],
            out_specs=pl.BlockSpec((TME, D), lambda j, f, te, nu: (j, 0)),
            scratch_shapes=[pltpu.VMEM((TME, D), F32), pltpu.VMEM((TME, D), BF16)]),
        out_shape=jax.ShapeDtypeStruct((P, D), F32),
        compiler_params=_cp(("arbitrary", "arbitrary")),
        name="moe_expert_ffn",
    )(tile_expert, n_used, xs, wg, wu, wd)


def _combine_kernel(dest_ref, y_ref, wts_ref, x1_ref, mod_ref, gpost_ref, gnext_ref, modn_ref,
                    x2_ref, hn_ref, ybuf, sem, *, T, TMC, D):
    base = pl.program_id(0) * TMC

    def row_copy(r, k):
        return pltpu.make_async_copy(y_ref.at[pl.ds(dest_ref[k * T + base + r], 1)],
                                     ybuf.at[k, pl.ds(r, 1)], sem)

    def start(r, c):
        for k in range(TOP_K):
            row_copy(r, k).start()
        return c

    def wait(r, c):
        for k in range(TOP_K):
            row_copy(r, k).wait()
        return c

    lax.fori_loop(0, TMC, start, 0)
    lax.fori_loop(0, TMC, wait, 0)
    w = wts_ref[...]
    y = w[:, 0:1] * ybuf[0] + w[:, 1:2] * ybuf[1]
    _ffn_epilogue(y, x1_ref, mod_ref, gpost_ref, gnext_ref, modn_ref, x2_ref, hn_ref, D)


def _combine(dest_flat, y, wts, x1, mod, gpost, gnext, modn, cfg):
    T, D = x1.shape
    TMC = cfg["TM_MERGE"]
    scale = cfg["TM"] // TMC
    row = lambda i, d: (i, 0)
    const = lambda i, d: (0, 0)
    modrow = lambda i, d: (cfg["mod_row"](i // scale), 0, 0)
    return pl.pallas_call(
        functools.partial(_combine_kernel, T=T, TMC=TMC, D=D),
        grid_spec=pltpu.PrefetchScalarGridSpec(
            num_scalar_prefetch=1,
            grid=(T // TMC,),
            in_specs=[pl.BlockSpec(memory_space=pl.ANY),
                      pl.BlockSpec((TMC, TOP_K), row),
                      pl.BlockSpec((TMC, D), row),
                      pl.BlockSpec((1, 1, 6 * D), modrow),
                      pl.BlockSpec((1, D), const), pl.BlockSpec((1, D), const),
                      pl.BlockSpec((1, 1, 6 * D), modrow)],
            out_specs=[pl.BlockSpec((TMC, D), row), pl.BlockSpec((TMC, D), row)],
            scratch_shapes=[pltpu.VMEM((TOP_K, TMC, D), F32), pltpu.SemaphoreType.DMA(())]),
        out_shape=[jax.ShapeDtypeStruct((T, D), F32), jax.ShapeDtypeStruct((T, D), BF16)],
        compiler_params=_cp(("arbitrary",)),
        name="moe_combine",
    )(dest_flat, y, wts, x1, mod, gpost, gnext, modn)


def _rope_tables(Ss, TM):
    n_freq = LANES // 4
    t = jnp.arange(Ss, dtype=I32)
    rows = (t // GRID_W).astype(F32)
    cols = (t % GRID_W).astype(F32)
    inv_freq = jnp.power(ROPE_THETA, -jnp.arange(n_freq, dtype=F32) / n_freq)
    ar = rows[:, None] * inv_freq
    ac = cols[:, None] * inv_freq
    cos = jnp.concatenate([jnp.cos(ar), jnp.cos(ar), jnp.cos(ac), jnp.cos(ac)], axis=-1)
    sin = jnp.concatenate([-jnp.sin(ar), jnp.sin(ar), -jnp.sin(ac), jnp.sin(ac)], axis=-1)
    cos = jnp.concatenate([jnp.ones((TM, LANES), F32), cos], axis=0)
    sin = jnp.concatenate([jnp.zeros((TM, LANES), F32), sin], axis=0)
    return cos, sin


def _dft_tables(n):
    j = jnp.arange(n, dtype=I32)
    ph = (j[:, None] * j[None, :]) % n
    ang = ph.astype(F32) * (2.0 * math.pi / n)
    return jnp.cos(ang).astype(BF16), jnp.sin(ang).astype(BF16)


def kernel(x_prompt, x_sample, cache_k, cache_v, c, c_ctx, w_mod, b_mod, g_pre_mix, g_post_mix,
           g_pre_ffn, g_post_ffn, w_in, g_q, g_k, w_pool_map, pool_scale, w_attn_o, w_pool_o,
           w_four_o, w_out, w_ffn_gate, w_ffn_up, w_ffn_down, w_router, w_exp_gate, w_exp_up,
           w_exp_down):
    Bp, Sp, D = x_prompt.shape
    Bs, Ss, _ = x_sample.shape
    L = w_mod.shape[0]
    PAST, KV = cache_k.shape[2], cache_k.shape[3]
    AW, PW, FW = w_attn_o.shape[1], w_pool_o.shape[1], w_four_o.shape[1]
    NH = AW // LANES
    G = NH // KV
    KW = KV * LANES
    E = w_router.shape[2]
    NP, NS = Bp * Sp, Bs * Ss
    T = NP + NS
    TM = min(512, math.gcd(NP, Ss))
    assert cache_k.shape[4] == LANES and NP % TM == 0 and Ss % TM == 0 and NP % Ss == 0
    assert Ss % Sp == 0 and Sp & (Sp - 1) == 0 and Ss & (Ss - 1) == 0 and Ss % GRID_W == 0
    NPT, TPS = NP // TM, Ss // TM
    TME = TM
    P = (-(-(T * TOP_K) // TME) + E) * TME

    cfg = dict(
        TM=TM, TM_MERGE=min(TM, 256), TME=TME, TF=512 if w_ffn_gate.shape[2] % 512 == 0 else 256,
        NH=NH, KV=KV, PW=PW, FW=FW, NP=NP, Sp=Sp, Ss=Ss,
        mod_row=lambda i: jnp.where(i < NPT, 0, 1 + (i - NPT) // TPS),
        rope_blk=lambda i: jnp.where(i < NPT, 0, 1 + (i - NPT) % TPS),
    )

    x = jnp.concatenate([x_prompt.reshape(NP, D), x_sample.reshape(NS, D)], axis=0)
    R = -(-(1 + Bs) // 8) * 8
    cond = jnp.zeros((R, D), F32).at[0].set(c_ctx).at[1:1 + Bs].set(c)
    mod_all = _modulation(cond, w_mod, b_mod).reshape(L, R, 1, 6 * D)

    o1, o2, o3, o4 = AW, AW + 2 * KW, AW + 2 * KW + PW, AW + 2 * KW + PW + FW
    w_in_b = w_in.astype(BF16)
    cos_tab, sin_tab = _rope_tables(Ss, TM)
    cc, sc = _dft_tables(LANES)
    cs_chan = jnp.concatenate([cc, sc], axis=1)
    dft_p = _dft_tables(Sp)
    dft_s = _dft_tables(Ss)
    ck = cache_k.astype(BF16).reshape(Bs, L, PAST, KW)
    cv = cache_v.astype(BF16).reshape(Bs, L, PAST, KW)
    row1 = lambda a: a.reshape(1, -1)

    h = _norm_mod(x, row1(g_pre_mix[0]), mod_all[0], cfg)
    new_k, new_v = [], []
    for l in range(L):
        mod = mod_all[l]
        q, kb, vb, kf, vf = _qkv(h, w_in_b[l, :, :o2], row1(g_q[l]), row1(g_k[l]), cos_tab, sin_tab, cfg)
        u_pool, fa, fb = _pool_fourier_proj(h, w_in_b[l, :, o2:o4], cs_chan, cfg)
        gates = _gates(h, w_in_b[l, :, o4:], cfg)
        new_k.append(kf[:NP].reshape(Bp, Sp, KV, LANES))
        new_v.append(vf[:NP].reshape(Bp, Sp, KV, LANES))

        attn_p = _attention(q, kb, vb, q_row0=0, B=Bp, S=Sp, Tk=Sp, KV=KV, G=G)
        k_s = jnp.concatenate([ck[:, l], kb[NP:].reshape(Bs, Ss, KW)], axis=1).reshape(-1, KW)
        v_s = jnp.concatenate([cv[:, l], vb[NP:].reshape(Bs, Ss, KW)], axis=1).reshape(-1, KW)
        attn_s = _attention(q, k_s, v_s, q_row0=NP, B=Bs, S=Ss, Tk=PAST + Ss, KV=KV, G=G)
        attn = jnp.concatenate([attn_p, attn_s], axis=0)

        pool = _pool(u_pool, w_pool_map[l].astype(BF16), row1(pool_scale[l]), cfg)
        four = jnp.concatenate([
            _fourier(fa, fb, *dft_p, row0=0, B=Bp, S=Sp),
            _fourier(fa, fb, *dft_s, row0=NP, B=Bs, S=Ss)], axis=0)

        x1, h2 = _merge(attn, pool, four, gates, x, mod, w_attn_o[l].astype(BF16),
                        w_pool_o[l].astype(BF16), w_four_o[l].astype(BF16), w_out[l].astype(BF16),
                        row1(g_post_mix[l]), row1(g_pre_ffn[l]), BF16 if l % 2 == 0 else F32, cfg)

        ln = min(l + 1, L - 1)
        gnext, modn = row1(g_pre_mix[ln]), mod_all[ln]
        i = l // 2
        if l % 2 == 0:
            x, h = _dense_ffn(h2, w_ffn_gate[i].astype(BF16), w_ffn_up[i].astype(BF16),
                              w_ffn_down[i].astype(BF16), x1, mod, row1(g_post_ffn[l]), gnext, modn, cfg)
        else:
            eidx, rank, wts, cnt = _router(h2, w_router[i].T, cfg)
            counts = cnt[:, 0]
            padded = (counts + TME - 1) // TME * TME
            pad_end = jnp.cumsum(padded)
            pad_start = pad_end - padded
            start_of = sum(jnp.where(eidx == e, pad_start[e], 0) for e in range(E))
            dest = (start_of + rank).reshape(-1).astype(I32)
            n_used = (pad_end[-1] // TME).astype(I32).reshape(1)
            tile_start = jnp.arange(P // TME, dtype=I32) * TME
            tile_expert = jnp.minimum(
                jnp.sum(tile_start[:, None] >= pad_end[None, :], axis=1), E - 1).astype(I32)
            xs = _dispatch(dest, h2, P, cfg)
            y = _moe_ffn(tile_expert, n_used, xs, w_exp_gate[i].astype(BF16),
                         w_exp_up[i].astype(BF16), w_exp_down[i].astype(BF16), cfg)
            x, h = _combine(dest, y, wts.T, x1, mod, row1(g_post_ffn[l]), gnext, modn, cfg)

    y_prompt = x[:NP].reshape(Bp, Sp, D)
    y_sample = x[NP:].reshape(Bs, Ss, D)
    return (y_prompt, y_sample, jnp.stack(new_k, axis=1), jnp.stack(new_v, axis=1))
```

```python
import functools
import math

import jax
import jax.numpy as jnp
from jax import lax
from jax.experimental import pallas as pl
from jax.experimental.pallas import tpu as pltpu

F32 = jnp.float32
BF16 = jnp.bfloat16
I32 = jnp.int32

LANES = 128
GRID_W = 64
ROPE_THETA = 10000.0
EPS = 1e-6
TOP_K = 2
POOL_WINDOWS = (2, 4, 8, 16)
VMEM_LIMIT = 56 * 1024 * 1024


def _cp(sem, vmem=VMEM_LIMIT):
    return pltpu.CompilerParams(dimension_semantics=sem, vmem_limit_bytes=vmem)


def _rms(x, g):
    return x * lax.rsqrt(jnp.mean(x * x, axis=-1, keepdims=True) + EPS) * g


def _silu(x):
    return x / (1.0 + jnp.exp(-x))


def _dot(a, b):
    return jnp.dot(a, b, preferred_element_type=F32)


def _resident(shape, index_map):
    return pl.BlockSpec(shape, index_map, pipeline_mode=pl.Buffered(1))


def _mod_kernel(c_ref, w_ref, b_ref, o_ref):
    s = _silu(c_ref[...])
    o_ref[0] = jnp.dot(s, w_ref[0], preferred_element_type=F32,
                       precision=lax.Precision.HIGHEST) + b_ref[0]


def _modulation(cond, w_mod, b_mod):
    L, D, N = w_mod.shape
    R = cond.shape[0]
    tn = 1024
    return pl.pallas_call(
        _mod_kernel,
        grid=(L, N // tn),
        in_specs=[pl.BlockSpec((R, D), lambda l, j: (0, 0)),
                  pl.BlockSpec((1, D, tn), lambda l, j: (l, 0, j)),
                  pl.BlockSpec((1, 1, tn), lambda l, j: (l, 0, j))],
        out_specs=pl.BlockSpec((1, R, tn), lambda l, j: (l, 0, j)),
        out_shape=jax.ShapeDtypeStruct((L, R, N), F32),
        compiler_params=_cp(("arbitrary", "arbitrary")),
        name="modulation",
    )(cond, w_mod, b_mod.reshape(L, 1, N))


def _norm_mod_kernel(x_ref, g_ref, mod_ref, h_ref, *, D):
    mod = mod_ref[0]
    y = _rms(x_ref[...], g_ref[...])
    h_ref[...] = (y * (1.0 + mod[:, D:2 * D]) + mod[:, 0:D]).astype(BF16)


def _norm_mod(x, g, mod, cfg):
    T, D = x.shape
    TM = cfg["TM"]
    return pl.pallas_call(
        functools.partial(_norm_mod_kernel, D=D),
        grid=(T // TM,),
        in_specs=[pl.BlockSpec((TM, D), lambda i: (i, 0)),
                  pl.BlockSpec((1, D), lambda i: (0, 0)),
                  pl.BlockSpec((1, 1, 6 * D), lambda i: (cfg["mod_row"](i), 0, 0))],
        out_specs=pl.BlockSpec((TM, D), lambda i: (i, 0)),
        out_shape=jax.ShapeDtypeStruct((T, D), BF16),
        compiler_params=_cp(("arbitrary",)),
        name="norm_mod",
    )(x, g, mod)


def _qkv_kernel(h_ref, w_ref, gq_ref, gk_ref, cos_ref, sin_ref,
                q_ref, kb_ref, vb_ref, kf_ref, vf_ref, *, NH, KV, scale):
    acc = _dot(h_ref[...], w_ref[...])
    cos = cos_ref[...]
    sin = sin_ref[...]
    lane = lax.broadcasted_iota(I32, cos.shape, 1)
    first_half = (lane % (LANES // 2)) < (LANES // 4)

    def rope(y):
        partner = jnp.where(first_half, pltpu.roll(y, LANES - LANES // 4, axis=1),
                            pltpu.roll(y, LANES // 4, axis=1))
        return y * cos + partner * sin

    for hd in range(NH):
        z = acc[:, hd * LANES:(hd + 1) * LANES]
        q_ref[:, hd * LANES:(hd + 1) * LANES] = (rope(_rms(z, gq_ref[...])) * scale).astype(BF16)
    for hd in range(KV):
        z = acc[:, (NH + hd) * LANES:(NH + hd + 1) * LANES]
        y = _rms(z, gk_ref[...])
        kf_ref[:, hd * LANES:(hd + 1) * LANES] = y
        kb_ref[:, hd * LANES:(hd + 1) * LANES] = rope(y).astype(BF16)
    v = acc[:, (NH + KV) * LANES:(NH + 2 * KV) * LANES]
    vf_ref[...] = v
    vb_ref[...] = v.astype(BF16)


def _qkv(h, w, gq, gk, cos_tab, sin_tab, cfg):
    T, D = h.shape
    TM, NH, KV = cfg["TM"], cfg["NH"], cfg["KV"]
    AW, KW = NH * LANES, KV * LANES
    row = lambda i: (i, 0)
    return pl.pallas_call(
        functools.partial(_qkv_kernel, NH=NH, KV=KV, scale=float(LANES) ** -0.5 * math.log2(math.e)),
        grid=(T // TM,),
        in_specs=[pl.BlockSpec((TM, D), row),
                  _resident((D, AW + 2 * KW), lambda i: (0, 0)),
                  pl.BlockSpec((1, LANES), lambda i: (0, 0)),
                  pl.BlockSpec((1, LANES), lambda i: (0, 0)),
                  pl.BlockSpec((TM, LANES), lambda i: (cfg["rope_blk"](i), 0)),
                  pl.BlockSpec((TM, LANES), lambda i: (cfg["rope_blk"](i), 0))],
        out_specs=[pl.BlockSpec((TM, AW), row), pl.BlockSpec((TM, KW), row),
                   pl.BlockSpec((TM, KW), row), pl.BlockSpec((TM, KW), row),
                   pl.BlockSpec((TM, KW), row)],
        out_shape=[jax.ShapeDtypeStruct((T, AW), BF16), jax.ShapeDtypeStruct((T, KW), BF16),
                   jax.ShapeDtypeStruct((T, KW), BF16), jax.ShapeDtypeStruct((T, KW), F32),
                   jax.ShapeDtypeStruct((T, KW), F32)],
        compiler_params=_cp(("arbitrary",)),
        name="qkv_proj",
    )(h, w, gq, gk, cos_tab, sin_tab)


def _pf_kernel(h_ref, w_ref, cs_ref, up_ref, a_ref, b_ref, *, PW, NG):
    acc = _dot(h_ref[...], w_ref[...])
    up_ref[...] = acc[:, :PW]
    for g in range(NG):
        u = acc[:, PW + g * LANES:PW + (g + 1) * LANES].astype(BF16)
        ab = _dot(u, cs_ref[...])
        a_ref[:, g * LANES:(g + 1) * LANES] = ab[:, :LANES].astype(BF16)
        b_ref[:, g * LANES:(g + 1) * LANES] = ab[:, LANES:].astype(BF16)


def _pool_fourier_proj(h, w, cs, cfg):
    T, D = h.shape
    TM, PW, FW = cfg["TM"], cfg["PW"], cfg["FW"]
    row = lambda i: (i, 0)
    return pl.pallas_call(
        functools.partial(_pf_kernel, PW=PW, NG=FW // LANES),
        grid=(T // TM,),
        in_specs=[pl.BlockSpec((TM, D), row),
                  _resident((D, PW + FW), lambda i: (0, 0)),
                  pl.BlockSpec((LANES, 2 * LANES), lambda i: (0, 0))],
        out_specs=[pl.BlockSpec((TM, PW), row), pl.BlockSpec((TM, FW), row),
                   pl.BlockSpec((TM, FW), row)],
        out_shape=[jax.ShapeDtypeStruct((T, PW), F32), jax.ShapeDtypeStruct((T, FW), BF16),
                   jax.ShapeDtypeStruct((T, FW), BF16)],
        compiler_params=_cp(("arbitrary",)),
        name="pool_fourier_proj",
    )(h, w, cs)


def _gates_kernel(h_ref, w_ref, o_ref):
    z = _dot(h_ref[...], w_ref[...])
    o_ref[...] = (1.0 / (1.0 + jnp.exp(-z))).astype(BF16)


def _gates(h, w, cfg):
    T, D = h.shape
    N = w.shape[1]
    TM = cfg["TM"]
    tn = 1024
    return pl.pallas_call(
        _gates_kernel,
        grid=(N // tn, T // TM),
        in_specs=[pl.BlockSpec((TM, D), lambda j, i: (i, 0)),
                  pl.BlockSpec((D, tn), lambda j, i: (0, j))],
        out_specs=pl.BlockSpec((TM, tn), lambda j, i: (i, j)),
        out_shape=jax.ShapeDtypeStruct((T, N), BF16),
        compiler_params=_cp(("arbitrary", "arbitrary")),
        name="gates_proj",
    )(h, w)


def _attn_kernel(q_ref, k_ref, v_ref, o_ref, *, G, tq, tk, nk):
    q = jnp.concatenate([q_ref[:, g * LANES:(g + 1) * LANES] for g in range(G)], axis=0)
    rows = G * tq
    nt = tk // LANES
    m = jnp.full((rows, LANES), -jnp.inf, F32)
    l = jnp.zeros((rows, LANES), F32)
    acc = jnp.zeros((rows, LANES), F32)
    for c in range(nk):
        k = k_ref[c * tk:(c + 1) * tk, :]
        v = v_ref[c * tk:(c + 1) * tk, :]
        s = lax.dot_general(q, k, (((1,), (1,)), ((), ())), preferred_element_type=F32)
        m_new = jnp.maximum(m, jnp.max(s, axis=-1, keepdims=True))
        alpha = jnp.exp2(m - m_new)
        p = [jnp.exp2(s[:, t * LANES:(t + 1) * LANES] - m_new) for t in range(nt)]
        l = alpha * l + functools.reduce(lambda a, b: a + b, p)
        pb = jnp.concatenate([pt.astype(BF16) for pt in p], axis=1)
        acc = alpha * acc + _dot(pb, v)
        m = m_new
    o = acc / jnp.sum(l, axis=-1, keepdims=True)
    for g in range(G):
        o_ref[:, g * LANES:(g + 1) * LANES] = o[g * tq:(g + 1) * tq].astype(BF16)


def _attention(q, k, v, *, q_row0, B, S, Tk, KV, G):
    tq = min(256, S)
    tk = next(t for t in (512, 256, 128) if Tk % t == 0)
    nq, nk = S // tq, Tk // tk
    qb0 = q_row0 // tq
    return pl.pallas_call(
        functools.partial(_attn_kernel, G=G, tq=tq, tk=tk, nk=nk),
        grid=(B, KV, nq),
        in_specs=[pl.BlockSpec((tq, G * LANES), lambda b, h, qi: (qb0 + b * nq + qi, h)),
                  pl.BlockSpec((Tk, LANES), lambda b, h, qi: (b, h)),
                  pl.BlockSpec((Tk, LANES), lambda b, h, qi: (b, h))],
        out_specs=pl.BlockSpec((tq, G * LANES), lambda b, h, qi: (b * nq + qi, h)),
        out_shape=jax.ShapeDtypeStruct((B * S, KV * G * LANES), BF16),
        compiler_params=_cp(("arbitrary",) * 3),
        name="attention",
    )(q, k, v)


def _pool_kernel(u_ref, wmap_ref, scale_ref, o_ref, tot_sc, *, R, n_prompt_blocks, Sp, Ss):
    blk = pl.program_id(0)
    g = pl.program_id(1)
    S = jnp.where(blk < n_prompt_blocks, Sp, Ss)
    half = jnp.left_shift(1, g)
    pos = lax.broadcasted_iota(I32, (R, LANES), 0) & (S - 1)

    def add_offsets(offsets):
        u = u_ref[...]
        tot = tot_sc[...]
        for k in offsets:
            valid = jnp.logical_and(pos + k >= 0, pos + k < S)
            tot = tot + jnp.where(valid, pltpu.roll(u, (-k) % R, axis=0), 0.0)
        tot_sc[...] = tot

    tot_sc[...] = u_ref[...]
    add_offsets([-1])
    for lvl in range(1, len(POOL_WINDOWS)):
        h0, h1 = POOL_WINDOWS[lvl - 1] // 2, POOL_WINDOWS[lvl] // 2

        @pl.when(g >= lvl)
        def _():
            add_offsets(list(range(-h1, -h0)) + list(range(h0, h1)))

    cnt = (jnp.minimum(pos + half, S) - jnp.maximum(pos - half, 0)).astype(F32)
    pooled = tot_sc[...] / cnt - u_ref[...]
    mixed = _dot(pooled.astype(BF16), wmap_ref[0]) * scale_ref[...]
    o_ref[...] = mixed.astype(BF16)


def _pool(u, wmap, scale, cfg):
    T, PW = u.shape
    R, NP, Sp, Ss = cfg["Ss"], cfg["NP"], cfg["Sp"], cfg["Ss"]
    return pl.pallas_call(
        functools.partial(_pool_kernel, R=R, n_prompt_blocks=NP // R, Sp=Sp, Ss=Ss),
        grid=(T // R, PW // LANES),
        in_specs=[pl.BlockSpec((R, LANES), lambda b, g: (b, g)),
                  pl.BlockSpec((1, LANES, LANES), lambda b, g: (g, 0, 0)),
                  pl.BlockSpec((1, LANES), lambda b, g: (0, g))],
        out_specs=pl.BlockSpec((R, LANES), lambda b, g: (b, g)),
        out_shape=jax.ShapeDtypeStruct((T, PW), BF16),
        scratch_shapes=[pltpu.VMEM((R, LANES), F32)],
        compiler_params=_cp(("arbitrary", "arbitrary")),
        name="multiscale_pool",
    )(u, wmap, scale)


def _fourier_kernel(cs_ref, ss_ref, a_ref, b_ref, o_ref, acc_sc, *, scale):
    k = pl.program_id(2)

    @pl.when(k == 0)
    def _():
        acc_sc[...] = jnp.zeros(acc_sc.shape, F32)

    acc_sc[...] += _dot(cs_ref[...], a_ref[...]) - _dot(ss_ref[...], b_ref[...])

    @pl.when(k == pl.num_programs(2) - 1)
    def _():
        o_ref[...] = (acc_sc[...] * scale).astype(BF16)


def _fourier(a, b, cs, ss, *, row0, B, S):
    FW = a.shape[1]
    tm = min(1024, S)
    tk = min(1024, S)
    nm, nk = S // tm, S // tk
    kb0 = row0 // tk
    return pl.pallas_call(
        functools.partial(_fourier_kernel, scale=float(S * LANES) ** -0.5),
        grid=(B, nm, nk),
        in_specs=[pl.BlockSpec((tm, tk), lambda bb, i, k: (i, k)),
                  pl.BlockSpec((tm, tk), lambda bb, i, k: (i, k)),
                  pl.BlockSpec((tk, FW), lambda bb, i, k: (kb0 + bb * nk + k, 0)),
                  pl.BlockSpec((tk, FW), lambda bb, i, k: (kb0 + bb * nk + k, 0))],
        out_specs=pl.BlockSpec((tm, FW), lambda bb, i, k: (bb * nm + i, 0)),
        out_shape=jax.ShapeDtypeStruct((B * S, FW), BF16),
        scratch_shapes=[pltpu.VMEM((tm, FW), F32)],
        compiler_params=_cp(("arbitrary",) * 3),
        name="fourier_mix",
    )(cs, ss, a, b)


def _merge_kernel(attn_ref, pool_ref, four_ref, gates_ref, x_ref, mod_ref, wa_ref, wp_ref, wf_ref,
                  wo_ref, gpost_ref, gpre_ref, x1_ref, h2_ref, *, D):
    mod = mod_ref[0]
    merged = (gates_ref[:, 0:D].astype(F32) * _dot(attn_ref[...], wa_ref[...])
              + gates_ref[:, D:2 * D].astype(F32) * _dot(pool_ref[...], wp_ref[...])
              + gates_ref[:, 2 * D:3 * D].astype(F32) * _dot(four_ref[...], wf_ref[...]))
    y = _dot(merged.astype(BF16), wo_ref[...])
    x1 = x_ref[...] + mod[:, 2 * D:3 * D] * _rms(y, gpost_ref[...])
    x1_ref[...] = x1
    h2 = _rms(x1, gpre_ref[...]) * (1.0 + mod[:, 4 * D:5 * D]) + mod[:, 3 * D:4 * D]
    h2_ref[...] = h2.astype(h2_ref.dtype)


def _merge(attn, pool, four, gates, x, mod, wa, wp, wf, wo, gpost, gpre, h2_dtype, cfg):
    T, D = x.shape
    TM = cfg["TM_MERGE"]
    scale = cfg["TM"] // TM
    row = lambda i: (i, 0)
    const = lambda i: (0, 0)
    return pl.pallas_call(
        functools.partial(_merge_kernel, D=D),
        grid=(T // TM,),
        in_specs=[pl.BlockSpec((TM, attn.shape[1]), row), pl.BlockSpec((TM, pool.shape[1]), row),
                  pl.BlockSpec((TM, four.shape[1]), row), pl.BlockSpec((TM, 3 * D), row),
                  pl.BlockSpec((TM, D), row),
                  pl.BlockSpec((1, 1, 6 * D), lambda i: (cfg["mod_row"](i // scale), 0, 0)),
                  _resident(wa.shape, const), _resident(wp.shape, const),
                  _resident(wf.shape, const), _resident(wo.shape, const),
                  pl.BlockSpec((1, D), const), pl.BlockSpec((1, D), const)],
        out_specs=[pl.BlockSpec((TM, D), row), pl.BlockSpec((TM, D), row)],
        out_shape=[jax.ShapeDtypeStruct((T, D), F32), jax.ShapeDtypeStruct((T, D), h2_dtype)],
        compiler_params=_cp(("arbitrary",)),
        name="merge_out_proj",
    )(attn, pool, four, gates, x, mod, wa, wp, wf, wo, gpost, gpre)


def _ffn_epilogue(y, x1_ref, mod_ref, gpost_ref, gnext_ref, modn_ref, x2_ref, hn_ref, D):
    mod = mod_ref[0]
    x2 = x1_ref[...] + mod[:, 5 * D:6 * D] * _rms(y, gpost_ref[...])
    x2_ref[...] = x2
    modn = modn_ref[0]
    hn = _rms(x2, gnext_ref[...]) * (1.0 + modn[:, D:2 * D]) + modn[:, 0:D]
    hn_ref[...] = hn.astype(BF16)


def _swiglu_mid(x, wgu_ref):
    ab = _dot(x, wgu_ref[...])
    tf = ab.shape[1] // 2
    return (_silu(ab[:, :tf]) * ab[:, tf:]).astype(BF16)


def _pair_gate_up(wg, wu, tf):
    lead, F = wg.shape[:-1], wg.shape[-1]
    pair = jnp.stack([wg.astype(BF16).reshape(*lead, F // tf, tf),
                      wu.astype(BF16).reshape(*lead, F // tf, tf)], axis=-2)
    return pair.reshape(*lead, 2 * F)


def _ffn_kernel(h_ref, wgu_ref, wd_ref, x1_ref, mod_ref, gpost_ref, gnext_ref, modn_ref,
                x2_ref, hn_ref, acc_sc, *, D):
    f = pl.program_id(1)

    @pl.when(f == 0)
    def _():
        acc_sc[...] = jnp.zeros(acc_sc.shape, F32)

    acc_sc[...] += _dot(_swiglu_mid(h_ref[...], wgu_ref), wd_ref[...])

    @pl.when(f == pl.num_programs(1) - 1)
    def _():
        _ffn_epilogue(acc_sc[...], x1_ref, mod_ref, gpost_ref, gnext_ref, modn_ref, x2_ref, hn_ref, D)


def _dense_ffn(h, wgu, wd, layer, x1, mod, gpost, gnext, modn, cfg):
    T, D = h.shape
    DFF = wd.shape[1]
    TM = cfg["TM"]
    tf = cfg["TF"]
    row = lambda i, f: (i, 0)
    const = lambda i, f: (0, 0)
    modrow = lambda i, f: (cfg["mod_row"](i), 0, 0)
    return pl.pallas_call(
        functools.partial(_ffn_kernel, D=D),
        grid=(T // TM, DFF // tf),
        in_specs=[pl.BlockSpec((TM, D), row),
                  pl.BlockSpec((None, D, 2 * tf), lambda i, f: (layer, 0, f)),
                  pl.BlockSpec((None, tf, D), lambda i, f: (layer, f, 0)),
                  pl.BlockSpec((TM, D), row),
                  pl.BlockSpec((1, 1, 6 * D), modrow),
                  pl.BlockSpec((1, D), const), pl.BlockSpec((1, D), const),
                  pl.BlockSpec((1, 1, 6 * D), modrow)],
        out_specs=[pl.BlockSpec((TM, D), row), pl.BlockSpec((TM, D), row)],
        out_shape=[jax.ShapeDtypeStruct((T, D), F32), jax.ShapeDtypeStruct((T, D), BF16)],
        scratch_shapes=[pltpu.VMEM((TM, D), F32)],
        compiler_params=_cp(("arbitrary", "arbitrary")),
        name="dense_ffn",
    )(h, wgu, wd, x1, mod, gpost, gnext, modn)


def _router_kernel(h_ref, wr_ref, eidx_ref, rank_ref, wts_ref, cnt_ref, base_sc, *, E, TMR):
    i = pl.program_id(0)

    @pl.when(i == 0)
    def _():
        base_sc[...] = jnp.zeros(base_sc.shape, F32)

    logits = lax.dot_general(wr_ref[...], h_ref[...].astype(F32), (((1,), (1,)), ((), ())),
                             preferred_element_type=F32, precision=lax.Precision.HIGHEST)
    e_iota = lax.broadcasted_iota(I32, (E, TMR), 0)
    m1 = jnp.max(logits, axis=0, keepdims=True)
    i1 = jnp.min(jnp.where(logits == m1, e_iota, E), axis=0, keepdims=True)
    rest = jnp.where(e_iota == i1, -jnp.inf, logits)
    m2 = jnp.max(rest, axis=0, keepdims=True)
    i2 = jnp.min(jnp.where(rest == m2, e_iota, E), axis=0, keepdims=True)
    ex = jnp.exp(m2 - m1)
    w1 = 1.0 / (1.0 + ex)
    w2 = ex / (1.0 + ex)

    oh1 = e_iota == i1
    oh2 = e_iota == i2
    oh = jnp.where(jnp.logical_or(oh1, oh2), 1.0, 0.0)
    tri = jnp.where(lax.broadcasted_iota(I32, (TMR, TMR), 0) < lax.broadcasted_iota(I32, (TMR, TMR), 1),
                    1.0, 0.0).astype(BF16)
    before = _dot(oh.astype(BF16), tri) + base_sc[...]
    r1 = jnp.sum(jnp.where(oh1, before, 0.0), axis=0, keepdims=True)
    r2 = jnp.sum(jnp.where(oh2, before, 0.0), axis=0, keepdims=True)
    base_sc[...] = base_sc[...] + jnp.sum(oh, axis=1, keepdims=True)

    eidx_ref[0:1, :] = i1
    eidx_ref[1:2, :] = i2
    rank_ref[0:1, :] = r1.astype(I32)
    rank_ref[1:2, :] = r2.astype(I32)
    wts_ref[0:1, :] = w1
    wts_ref[1:2, :] = w2
    cnt_ref[...] = jnp.broadcast_to(base_sc[...], cnt_ref.shape).astype(I32)


def _router(h, wr_t, cfg):
    T, D = h.shape
    E = wr_t.shape[0]
    TMR = cfg["TM"]
    col = lambda i: (0, i)
    return pl.pallas_call(
        functools.partial(_router_kernel, E=E, TMR=TMR),
        grid=(T // TMR,),
        in_specs=[pl.BlockSpec((TMR, D), lambda i: (i, 0)),
                  pl.BlockSpec((E, D), lambda i: (0, 0))],
        out_specs=[pl.BlockSpec((TOP_K, TMR), col), pl.BlockSpec((TOP_K, TMR), col),
                   pl.BlockSpec((TOP_K, TMR), col), pl.BlockSpec((E, LANES), lambda i: (0, 0))],
        out_shape=[jax.ShapeDtypeStruct((TOP_K, T), I32), jax.ShapeDtypeStruct((TOP_K, T), I32),
                   jax.ShapeDtypeStruct((TOP_K, T), F32), jax.ShapeDtypeStruct((E, LANES), I32)],
        scratch_shapes=[pltpu.VMEM((E, 1), F32)],
        compiler_params=_cp(("arbitrary",)),
        name="moe_router",
    )(h, wr_t)


ROW_DMA_UNROLL = 8


def _row_dma_burst(row_copy, n_rows):
    def start(rb, c):
        for u in range(ROW_DMA_UNROLL):
            for k in range(TOP_K):
                row_copy(rb * ROW_DMA_UNROLL + u, k).start(priority=(u * TOP_K + k) % 2)
        return c

    def wait(rb, c):
        for u in range(ROW_DMA_UNROLL):
            for k in range(TOP_K):
                row_copy(rb * ROW_DMA_UNROLL + u, k).wait()
        return c

    lax.fori_loop(0, n_rows // ROW_DMA_UNROLL, start, 0)
    lax.fori_loop(0, n_rows // ROW_DMA_UNROLL, wait, 0)


def _dispatch_kernel(dest_ref, h_ref, xs_in_ref, xs_ref, sem, *, T, TMD):
    del xs_in_ref
    base = pl.program_id(0) * TMD

    def row_copy(r, k):
        return pltpu.make_async_copy(h_ref.at[pl.ds(r, 1)],
                                     xs_ref.at[pl.ds(dest_ref[k * T + base + r], 1)], sem)

    _row_dma_burst(row_copy, TMD)


def _dispatch(dest_flat, h, P, cfg):
    T, D = h.shape
    TMD = cfg["TM"]
    xs0 = jnp.zeros((P, D), h.dtype)
    return pl.pallas_call(
        functools.partial(_dispatch_kernel, T=T, TMD=TMD),
        grid_spec=pltpu.PrefetchScalarGridSpec(
            num_scalar_prefetch=1,
            grid=(T // TMD,),
            in_specs=[pl.BlockSpec((TMD, D), lambda i, dest: (i, 0)),
                      pl.BlockSpec(memory_space=pl.ANY)],
            out_specs=pl.BlockSpec(memory_space=pl.ANY),
            scratch_shapes=[pltpu.SemaphoreType.DMA(())]),
        out_shape=jax.ShapeDtypeStruct((P, D), h.dtype),
        input_output_aliases={2: 0},
        compiler_params=pltpu.CompilerParams(dimension_semantics=("arbitrary",),
                                             has_side_effects=True),
        name="moe_dispatch",
    )(dest_flat, h, xs0)


def _moe_kernel(te_ref, nu_ref, xs_ref, wgu_ref, wd_ref, y_ref, acc_sc, xb_sc):
    j = pl.program_id(0)
    f = pl.program_id(1)
    used = j < nu_ref[0]
    last = f == pl.num_programs(1) - 1

    @pl.when(jnp.logical_and(used, f == 0))
    def _():
        acc_sc[...] = jnp.zeros(acc_sc.shape, F32)
        xb_sc[...] = xs_ref[...].astype(BF16)

    @pl.when(used)
    def _():
        acc_sc[...] += _dot(_swiglu_mid(xb_sc[...], wgu_ref), wd_ref[...])

    @pl.when(jnp.logical_and(used, last))
    def _():
        y_ref[...] = acc_sc[...]

    @pl.when(jnp.logical_and(jnp.logical_not(used), last))
    def _():
        y_ref[...] = jnp.zeros(y_ref.shape, F32)


def _moe_ffn(tile_expert, n_used, xs, wgu, wd, layer, cfg):
    P, D = xs.shape
    DFE = wd.shape[2]
    TME, tf = cfg["TME"], cfg["TFE"]
    nf = DFE // tf

    def jj(j, nu):
        return jnp.minimum(j, nu[0] - 1)

    def ff(j, f, nu):
        return jnp.where(j < nu[0], f, nf - 1)

    return pl.pallas_call(
        _moe_kernel,
        grid_spec=pltpu.PrefetchScalarGridSpec(
            num_scalar_prefetch=2,
            grid=(P // TME, nf),
            in_specs=[pl.BlockSpec((TME, D), lambda j, f, te, nu: (jj(j, nu), 0)),
                      pl.BlockSpec((None, None, D, 2 * tf),
                                   lambda j, f, te, nu: (layer, te[jj(j, nu)], 0, ff(j, f, nu))),
                      pl.BlockSpec((None, None, tf, D),
                                   lambda j, f, te, nu: (layer, te[jj(j, nu)], ff(j, f, nu), 0))],
            out_specs=pl.BlockSpec((TME, D), lambda j, f, te, nu: (j, 0)),
            scratch_shapes=[pltpu.VMEM((TME, D), F32), pltpu.VMEM((TME, D), BF16)]),
        out_shape=jax.ShapeDtypeStruct((P, D), F32),
        compiler_params=_cp(("arbitrary", "arbitrary")),
        name="moe_expert_ffn",
    )(tile_expert, n_used, xs, wgu, wd)


def _combine_kernel(dest_ref, y_ref, wts_ref, x1_ref, mod_ref, gpost_ref, gnext_ref, modn_ref,
                    x2_ref, hn_ref, ybuf, sem, *, T, TMC, D):
    base = pl.program_id(0) * TMC

    def row_copy(r, k):
        return pltpu.make_async_copy(y_ref.at[pl.ds(dest_ref[k * T + base + r], 1)],
                                     ybuf.at[k, pl.ds(r, 1)], sem)

    _row_dma_burst(row_copy, TMC)
    w = wts_ref[...]
    y = w[:, 0:1] * ybuf[0] + w[:, 1:2] * ybuf[1]
    _ffn_epilogue(y, x1_ref, mod_ref, gpost_ref, gnext_ref, modn_ref, x2_ref, hn_ref, D)


def _combine(dest_flat, y, wts, x1, mod, gpost, gnext, modn, cfg):
    T, D = x1.shape
    TMC = cfg["TM_MERGE"]
    scale = cfg["TM"] // TMC
    row = lambda i, d: (i, 0)
    const = lambda i, d: (0, 0)
    modrow = lambda i, d: (cfg["mod_row"](i // scale), 0, 0)
    return pl.pallas_call(
        functools.partial(_combine_kernel, T=T, TMC=TMC, D=D),
        grid_spec=pltpu.PrefetchScalarGridSpec(
            num_scalar_prefetch=1,
            grid=(T // TMC,),
            in_specs=[pl.BlockSpec(memory_space=pl.ANY),
                      pl.BlockSpec((TMC, TOP_K), row),
                      pl.BlockSpec((TMC, D), row),
                      pl.BlockSpec((1, 1, 6 * D), modrow),
                      pl.BlockSpec((1, D), const), pl.BlockSpec((1, D), const),
                      pl.BlockSpec((1, 1, 6 * D), modrow)],
            out_specs=[pl.BlockSpec((TMC, D), row), pl.BlockSpec((TMC, D), row)],
            scratch_shapes=[pltpu.VMEM((TOP_K, TMC, D), F32), pltpu.SemaphoreType.DMA(())]),
        out_shape=[jax.ShapeDtypeStruct((T, D), F32), jax.ShapeDtypeStruct((T, D), BF16)],
        compiler_params=_cp(("arbitrary",)),
        name="moe_combine",
    )(dest_flat, y, wts, x1, mod, gpost, gnext, modn)


def _rope_tables(Ss, TM):
    n_freq = LANES // 4
    t = jnp.arange(Ss, dtype=I32)
    rows = (t // GRID_W).astype(F32)
    cols = (t % GRID_W).astype(F32)
    inv_freq = jnp.power(ROPE_THETA, -jnp.arange(n_freq, dtype=F32) / n_freq)
    ar = rows[:, None] * inv_freq
    ac = cols[:, None] * inv_freq
    cos = jnp.concatenate([jnp.cos(ar), jnp.cos(ar), jnp.cos(ac), jnp.cos(ac)], axis=-1)
    sin = jnp.concatenate([-jnp.sin(ar), jnp.sin(ar), -jnp.sin(ac), jnp.sin(ac)], axis=-1)
    cos = jnp.concatenate([jnp.ones((TM, LANES), F32), cos], axis=0)
    sin = jnp.concatenate([jnp.zeros((TM, LANES), F32), sin], axis=0)
    return cos, sin


def _dft_tables(n):
    j = jnp.arange(n, dtype=I32)
    ph = (j[:, None] * j[None, :]) % n
    ang = ph.astype(F32) * (2.0 * math.pi / n)
    return jnp.cos(ang).astype(BF16), jnp.sin(ang).astype(BF16)


def kernel(x_prompt, x_sample, cache_k, cache_v, c, c_ctx, w_mod, b_mod, g_pre_mix, g_post_mix,
           g_pre_ffn, g_post_ffn, w_in, g_q, g_k, w_pool_map, pool_scale, w_attn_o, w_pool_o,
           w_four_o, w_out, w_ffn_gate, w_ffn_up, w_ffn_down, w_router, w_exp_gate, w_exp_up,
           w_exp_down):
    Bp, Sp, D = x_prompt.shape
    Bs, Ss, _ = x_sample.shape
    L = w_mod.shape[0]
    PAST, KV = cache_k.shape[2], cache_k.shape[3]
    AW, PW, FW = w_attn_o.shape[1], w_pool_o.shape[1], w_four_o.shape[1]
    NH = AW // LANES
    G = NH // KV
    KW = KV * LANES
    E = w_router.shape[2]
    NP, NS = Bp * Sp, Bs * Ss
    T = NP + NS
    TM = min(512, math.gcd(NP, Ss))
    assert cache_k.shape[4] == LANES and NP % TM == 0 and Ss % TM == 0 and NP % Ss == 0
    assert Ss % Sp == 0 and Sp & (Sp - 1) == 0 and Ss & (Ss - 1) == 0 and Ss % GRID_W == 0
    NPT, TPS = NP // TM, Ss // TM
    TME = TM
    P = (-(-(T * TOP_K) // TME) + E) * TME

    pick_tile = lambda n, cands: next(t for t in cands if n % t == 0)
    cfg = dict(
        TM=TM, TM_MERGE=min(TM, 256), TME=TME,
        TF=pick_tile(w_ffn_gate.shape[2], (512, 256, 128)),
        TFE=pick_tile(w_exp_gate.shape[3], (1024, 512, 256, 128)),
        NH=NH, KV=KV, PW=PW, FW=FW, NP=NP, Sp=Sp, Ss=Ss,
        mod_row=lambda i: jnp.where(i < NPT, 0, 1 + (i - NPT) // TPS),
        rope_blk=lambda i: jnp.where(i < NPT, 0, 1 + (i - NPT) % TPS),
    )

    x = jnp.concatenate([x_prompt.reshape(NP, D), x_sample.reshape(NS, D)], axis=0)
    R = -(-(1 + Bs) // 8) * 8
    cond = jnp.zeros((R, D), F32).at[0].set(c_ctx).at[1:1 + Bs].set(c)
    mod_all = _modulation(cond, w_mod, b_mod).reshape(L, R, 1, 6 * D)

    o1, o2, o3, o4 = AW, AW + 2 * KW, AW + 2 * KW + PW, AW + 2 * KW + PW + FW
    w_in_b = w_in.astype(BF16)
    cos_tab, sin_tab = _rope_tables(Ss, TM)
    cc, sc = _dft_tables(LANES)
    cs_chan = jnp.concatenate([cc, sc], axis=1)
    dft_p = _dft_tables(Sp)
    dft_s = _dft_tables(Ss)
    ck = cache_k.astype(BF16).reshape(Bs, L, PAST, KW)
    cv = cache_v.astype(BF16).reshape(Bs, L, PAST, KW)
    row1 = lambda a: a.reshape(1, -1)
    wgu_dense = _pair_gate_up(w_ffn_gate, w_ffn_up, cfg["TF"])
    wd_dense = w_ffn_down.astype(BF16)
    wgu_exp = _pair_gate_up(w_exp_gate, w_exp_up, cfg["TFE"])
    wd_exp = w_exp_down.astype(BF16)

    h = _norm_mod(x, row1(g_pre_mix[0]), mod_all[0], cfg)
    new_k, new_v = [], []
    for l in range(L):
        mod = mod_all[l]
        q, kb, vb, kf, vf = _qkv(h, w_in_b[l, :, :o2], row1(g_q[l]), row1(g_k[l]), cos_tab, sin_tab, cfg)
        u_pool, fa, fb = _pool_fourier_proj(h, w_in_b[l, :, o2:o4], cs_chan, cfg)
        gates = _gates(h, w_in_b[l, :, o4:], cfg)
        new_k.append(kf[:NP].reshape(Bp, Sp, KV, LANES))
        new_v.append(vf[:NP].reshape(Bp, Sp, KV, LANES))

        attn_p = _attention(q, kb, vb, q_row0=0, B=Bp, S=Sp, Tk=Sp, KV=KV, G=G)
        k_s = jnp.concatenate([ck[:, l], kb[NP:].reshape(Bs, Ss, KW)], axis=1).reshape(-1, KW)
        v_s = jnp.concatenate([cv[:, l], vb[NP:].reshape(Bs, Ss, KW)], axis=1).reshape(-1, KW)
        attn_s = _attention(q, k_s, v_s, q_row0=NP, B=Bs, S=Ss, Tk=PAST + Ss, KV=KV, G=G)
        attn = jnp.concatenate([attn_p, attn_s], axis=0)

        pool = _pool(u_pool, w_pool_map[l].astype(BF16), row1(pool_scale[l]), cfg)
        four = jnp.concatenate([
            _fourier(fa, fb, *dft_p, row0=0, B=Bp, S=Sp),
            _fourier(fa, fb, *dft_s, row0=NP, B=Bs, S=Ss)], axis=0)

        x1, h2 = _merge(attn, pool, four, gates, x, mod, w_attn_o[l].astype(BF16),
                        w_pool_o[l].astype(BF16), w_four_o[l].astype(BF16), w_out[l].astype(BF16),
                        row1(g_post_mix[l]), row1(g_pre_ffn[l]), BF16 if l % 2 == 0 else F32, cfg)

        ln = min(l + 1, L - 1)
        gnext, modn = row1(g_pre_mix[ln]), mod_all[ln]
        i = l // 2
        if l % 2 == 0:
            x, h = _dense_ffn(h2, wgu_dense, wd_dense, i, x1, mod, row1(g_post_ffn[l]), gnext, modn, cfg)
        else:
            eidx, rank, wts, cnt = _router(h2, w_router[i].T, cfg)
            counts = cnt[:, 0]
            padded = (counts + TME - 1) // TME * TME
            pad_end = jnp.cumsum(padded)
            pad_start = pad_end - padded
            start_of = sum(jnp.where(eidx == e, pad_start[e], 0) for e in range(E))
            dest = (start_of + rank).reshape(-1).astype(I32)
            n_used = (pad_end[-1] // TME).astype(I32).reshape(1)
            tile_start = jnp.arange(P // TME, dtype=I32) * TME
            tile_expert = jnp.minimum(
                jnp.sum(tile_start[:, None] >= pad_end[None, :], axis=1), E - 1).astype(I32)
            xs = _dispatch(dest, h2, P, cfg)
            y = _moe_ffn(tile_expert, n_used, xs, wgu_exp, wd_exp, i, cfg)
            x, h = _combine(dest, y, wts.T, x1, mod, row1(g_post_ffn[l]), gnext, modn, cfg)

    y_prompt = x[:NP].reshape(Bp, Sp, D)
    y_sample = x[NP:].reshape(Bs, Ss, D)
    return (y_prompt, y_sample, jnp.stack(new_k, axis=1), jnp.stack(new_v, axis=1))
```

```python
import functools
import math

import jax
import jax.numpy as jnp
from jax import lax
from jax.experimental import pallas as pl
from jax.experimental.pallas import tpu as pltpu

F32 = jnp.float32
BF16 = jnp.bfloat16
I32 = jnp.int32

LANES = 128
GRID_W = 64
ROPE_THETA = 10000.0
EPS = 1e-6
TOP_K = 2
POOL_WINDOWS = (2, 4, 8, 16)
VMEM_LIMIT = 56 * 1024 * 1024


def _cp(sem, vmem=VMEM_LIMIT):
    return pltpu.CompilerParams(dimension_semantics=sem, vmem_limit_bytes=vmem)


def _rms(x, g):
    return x * lax.rsqrt(jnp.mean(x * x, axis=-1, keepdims=True) + EPS) * g


def _silu(x):
    return x / (1.0 + jnp.exp(-x))


def _dot(a, b):
    return jnp.dot(a, b, preferred_element_type=F32)


def _resident(shape, index_map):
    return pl.BlockSpec(shape, index_map, pipeline_mode=pl.Buffered(1))


def _mod_kernel(c_ref, w_ref, b_ref, o_ref):
    s = _silu(c_ref[...])
    o_ref[0] = jnp.dot(s, w_ref[0], preferred_element_type=F32,
                       precision=lax.Precision.HIGHEST) + b_ref[0]


def _modulation(cond, w_mod, b_mod):
    L, D, N = w_mod.shape
    R = cond.shape[0]
    tn = 1024
    return pl.pallas_call(
        _mod_kernel,
        grid=(L, N // tn),
        in_specs=[pl.BlockSpec((R, D), lambda l, j: (0, 0)),
                  pl.BlockSpec((1, D, tn), lambda l, j: (l, 0, j)),
                  pl.BlockSpec((1, 1, tn), lambda l, j: (l, 0, j))],
        out_specs=pl.BlockSpec((1, R, tn), lambda l, j: (l, 0, j)),
        out_shape=jax.ShapeDtypeStruct((L, R, N), F32),
        compiler_params=_cp(("arbitrary", "arbitrary")),
        name="modulation",
    )(cond, w_mod, b_mod.reshape(L, 1, N))


def _norm_mod_kernel(x_ref, g_ref, mod_ref, h_ref, *, D):
    mod = mod_ref[0]
    y = _rms(x_ref[...], g_ref[...])
    h_ref[...] = (y * (1.0 + mod[:, D:2 * D]) + mod[:, 0:D]).astype(BF16)


def _norm_mod(x, g, mod, cfg):
    T, D = x.shape
    TM = cfg["TM"]
    return pl.pallas_call(
        functools.partial(_norm_mod_kernel, D=D),
        grid=(T // TM,),
        in_specs=[pl.BlockSpec((TM, D), lambda i: (i, 0)),
                  pl.BlockSpec((1, D), lambda i: (0, 0)),
                  pl.BlockSpec((1, 1, 6 * D), lambda i: (cfg["mod_row"](i), 0, 0))],
        out_specs=pl.BlockSpec((TM, D), lambda i: (i, 0)),
        out_shape=jax.ShapeDtypeStruct((T, D), BF16),
        compiler_params=_cp(("arbitrary",)),
        name="norm_mod",
    )(x, g, mod)


def _qkv_kernel(h_ref, w_ref, gq_ref, gk_ref, cos_ref, sin_ref,
                q_ref, kb_ref, vb_ref, kf_ref, vf_ref, *, NH, KV, scale):
    acc = _dot(h_ref[...], w_ref[...])
    cos = cos_ref[...]
    sin = sin_ref[...]
    lane = lax.broadcasted_iota(I32, cos.shape, 1)
    first_half = (lane % (LANES // 2)) < (LANES // 4)

    def rope(y):
        partner = jnp.where(first_half, pltpu.roll(y, LANES - LANES // 4, axis=1),
                            pltpu.roll(y, LANES // 4, axis=1))
        return y * cos + partner * sin

    for hd in range(NH):
        z = acc[:, hd * LANES:(hd + 1) * LANES]
        q_ref[:, hd * LANES:(hd + 1) * LANES] = (rope(_rms(z, gq_ref[...])) * scale).astype(BF16)
    for hd in range(KV):
        z = acc[:, (NH + hd) * LANES:(NH + hd + 1) * LANES]
        y = _rms(z, gk_ref[...])
        kf_ref[:, hd * LANES:(hd + 1) * LANES] = y
        kb_ref[:, hd * LANES:(hd + 1) * LANES] = rope(y).astype(BF16)
    v = acc[:, (NH + KV) * LANES:(NH + 2 * KV) * LANES]
    vf_ref[...] = v
    vb_ref[...] = v.astype(BF16)


def _qkv(h, w, gq, gk, cos_tab, sin_tab, cfg):
    T, D = h.shape
    TM, NH, KV = cfg["TM"], cfg["NH"], cfg["KV"]
    AW, KW = NH * LANES, KV * LANES
    row = lambda i: (i, 0)
    return pl.pallas_call(
        functools.partial(_qkv_kernel, NH=NH, KV=KV, scale=float(LANES) ** -0.5 * math.log2(math.e)),
        grid=(T // TM,),
        in_specs=[pl.BlockSpec((TM, D), row),
                  _resident((D, AW + 2 * KW), lambda i: (0, 0)),
                  pl.BlockSpec((1, LANES), lambda i: (0, 0)),
                  pl.BlockSpec((1, LANES), lambda i: (0, 0)),
                  pl.BlockSpec((TM, LANES), lambda i: (cfg["rope_blk"](i), 0)),
                  pl.BlockSpec((TM, LANES), lambda i: (cfg["rope_blk"](i), 0))],
        out_specs=[pl.BlockSpec((TM, AW), row), pl.BlockSpec((TM, KW), row),
                   pl.BlockSpec((TM, KW), row), pl.BlockSpec((TM, KW), row),
                   pl.BlockSpec((TM, KW), row)],
        out_shape=[jax.ShapeDtypeStruct((T, AW), BF16), jax.ShapeDtypeStruct((T, KW), BF16),
                   jax.ShapeDtypeStruct((T, KW), BF16), jax.ShapeDtypeStruct((T, KW), F32),
                   jax.ShapeDtypeStruct((T, KW), F32)],
        compiler_params=_cp(("arbitrary",)),
        name="qkv_proj",
    )(h, w, gq, gk, cos_tab, sin_tab)


def _pf_kernel(h_ref, w_ref, cs_ref, up_ref, a_ref, b_ref, *, PW, NG):
    acc = _dot(h_ref[...], w_ref[...])
    up_ref[...] = acc[:, :PW]
    for g in range(NG):
        u = acc[:, PW + g * LANES:PW + (g + 1) * LANES].astype(BF16)
        ab = _dot(u, cs_ref[...])
        a_ref[:, g * LANES:(g + 1) * LANES] = ab[:, :LANES].astype(BF16)
        b_ref[:, g * LANES:(g + 1) * LANES] = ab[:, LANES:].astype(BF16)


def _pool_fourier_proj(h, w, cs, cfg):
    T, D = h.shape
    TM, PW, FW = cfg["TM"], cfg["PW"], cfg["FW"]
    row = lambda i: (i, 0)
    return pl.pallas_call(
        functools.partial(_pf_kernel, PW=PW, NG=FW // LANES),
        grid=(T // TM,),
        in_specs=[pl.BlockSpec((TM, D), row),
                  _resident((D, PW + FW), lambda i: (0, 0)),
                  pl.BlockSpec((LANES, 2 * LANES), lambda i: (0, 0))],
        out_specs=[pl.BlockSpec((TM, PW), row), pl.BlockSpec((TM, FW), row),
                   pl.BlockSpec((TM, FW), row)],
        out_shape=[jax.ShapeDtypeStruct((T, PW), F32), jax.ShapeDtypeStruct((T, FW), BF16),
                   jax.ShapeDtypeStruct((T, FW), BF16)],
        compiler_params=_cp(("arbitrary",)),
        name="pool_fourier_proj",
    )(h, w, cs)


def _gates_kernel(h_ref, w_ref, o_ref):
    z = _dot(h_ref[...], w_ref[...])
    o_ref[...] = (1.0 / (1.0 + jnp.exp(-z))).astype(BF16)


def _gates(h, w, cfg):
    T, D = h.shape
    N = w.shape[1]
    TM = cfg["TM_GATES"]
    tn = 1024
    return pl.pallas_call(
        _gates_kernel,
        grid=(N // tn, T // TM),
        in_specs=[pl.BlockSpec((TM, D), lambda j, i: (i, 0)),
                  pl.BlockSpec((D, tn), lambda j, i: (0, j))],
        out_specs=pl.BlockSpec((TM, tn), lambda j, i: (i, j)),
        out_shape=jax.ShapeDtypeStruct((T, N), BF16),
        compiler_params=_cp(("arbitrary", "arbitrary")),
        name="gates_proj",
    )(h, w)


def _attn_kernel(q_ref, k_ref, v_ref, o_ref, *, G, tq, tk, nk):
    q = jnp.concatenate([q_ref[:, g * LANES:(g + 1) * LANES] for g in range(G)], axis=0)
    rows = G * tq
    nt = tk // LANES
    m = jnp.full((rows, LANES), -jnp.inf, F32)
    l = jnp.zeros((rows, LANES), F32)
    acc = jnp.zeros((rows, LANES), F32)
    for c in range(nk):
        k = k_ref[c * tk:(c + 1) * tk, :]
        v = v_ref[c * tk:(c + 1) * tk, :]
        s = lax.dot_general(q, k, (((1,), (1,)), ((), ())), preferred_element_type=F32)
        m_new = jnp.maximum(m, jnp.max(s, axis=-1, keepdims=True))
        alpha = jnp.exp2(m - m_new)
        p = [jnp.exp2(s[:, t * LANES:(t + 1) * LANES] - m_new) for t in range(nt)]
        l = alpha * l + functools.reduce(lambda a, b: a + b, p)
        pb = jnp.concatenate([pt.astype(BF16) for pt in p], axis=1)
        acc = alpha * acc + _dot(pb, v)
        m = m_new
    o = acc / jnp.sum(l, axis=-1, keepdims=True)
    for g in range(G):
        o_ref[:, g * LANES:(g + 1) * LANES] = o[g * tq:(g + 1) * tq].astype(BF16)


def _attention(q, k, v, *, q_row0, B, S, Tk, KV, G):
    tq = min(256, S)
    tk = next(t for t in (512, 256, 128) if Tk % t == 0)
    nq, nk = S // tq, Tk // tk
    qb0 = q_row0 // tq
    return pl.pallas_call(
        functools.partial(_attn_kernel, G=G, tq=tq, tk=tk, nk=nk),
        grid=(B, KV, nq),
        in_specs=[pl.BlockSpec((tq, G * LANES), lambda b, h, qi: (qb0 + b * nq + qi, h)),
                  pl.BlockSpec((Tk, LANES), lambda b, h, qi: (b, h)),
                  pl.BlockSpec((Tk, LANES), lambda b, h, qi: (b, h))],
        out_specs=pl.BlockSpec((tq, G * LANES), lambda b, h, qi: (b * nq + qi, h)),
        out_shape=jax.ShapeDtypeStruct((B * S, KV * G * LANES), BF16),
        compiler_params=_cp(("arbitrary",) * 3),
        name="attention",
    )(q, k, v)


def _pool_kernel(u_ref, wmap_ref, scale_ref, o_ref, tot_sc, *, R, n_prompt_blocks, Sp, Ss):
    blk = pl.program_id(0)
    g = pl.program_id(1)
    S = jnp.where(blk < n_prompt_blocks, Sp, Ss)
    half = jnp.left_shift(1, g)
    pos = lax.broadcasted_iota(I32, (R, LANES), 0) & (S - 1)

    def add_offsets(offsets):
        u = u_ref[...]
        tot = tot_sc[...]
        for k in offsets:
            valid = jnp.logical_and(pos + k >= 0, pos + k < S)
            tot = tot + jnp.where(valid, pltpu.roll(u, (-k) % R, axis=0), 0.0)
        tot_sc[...] = tot

    tot_sc[...] = u_ref[...]
    add_offsets([-1])
    for lvl in range(1, len(POOL_WINDOWS)):
        h0, h1 = POOL_WINDOWS[lvl - 1] // 2, POOL_WINDOWS[lvl] // 2

        @pl.when(g >= lvl)
        def _():
            add_offsets(list(range(-h1, -h0)) + list(range(h0, h1)))

    cnt = (jnp.minimum(pos + half, S) - jnp.maximum(pos - half, 0)).astype(F32)
    pooled = tot_sc[...] / cnt - u_ref[...]
    mixed = _dot(pooled.astype(BF16), wmap_ref[0]) * scale_ref[...]
    o_ref[...] = mixed.astype(BF16)


def _pool(u, wmap, scale, cfg):
    T, PW = u.shape
    R, NP, Sp, Ss = cfg["Ss"], cfg["NP"], cfg["Sp"], cfg["Ss"]
    return pl.pallas_call(
        functools.partial(_pool_kernel, R=R, n_prompt_blocks=NP // R, Sp=Sp, Ss=Ss),
        grid=(T // R, PW // LANES),
        in_specs=[pl.BlockSpec((R, LANES), lambda b, g: (b, g)),
                  pl.BlockSpec((1, LANES, LANES), lambda b, g: (g, 0, 0)),
                  pl.BlockSpec((1, LANES), lambda b, g: (0, g))],
        out_specs=pl.BlockSpec((R, LANES), lambda b, g: (b, g)),
        out_shape=jax.ShapeDtypeStruct((T, PW), BF16),
        scratch_shapes=[pltpu.VMEM((R, LANES), F32)],
        compiler_params=_cp(("arbitrary", "arbitrary")),
        name="multiscale_pool",
    )(u, wmap, scale)


def _fourier_kernel(cs_ref, ss_ref, a_ref, b_ref, o_ref, acc_sc, *, scale):
    k = pl.program_id(2)

    @pl.when(k == 0)
    def _():
        acc_sc[...] = jnp.zeros(acc_sc.shape, F32)

    acc_sc[...] += _dot(cs_ref[...], a_ref[...]) - _dot(ss_ref[...], b_ref[...])

    @pl.when(k == pl.num_programs(2) - 1)
    def _():
        o_ref[...] = (acc_sc[...] * scale).astype(BF16)


def _fourier(a, b, cs, ss, *, row0, B, S):
    FW = a.shape[1]
    tm = min(1024, S)
    tk = min(1024, S)
    nm, nk = S // tm, S // tk
    kb0 = row0 // tk
    return pl.pallas_call(
        functools.partial(_fourier_kernel, scale=float(S * LANES) ** -0.5),
        grid=(B, nm, nk),
        in_specs=[pl.BlockSpec((tm, tk), lambda bb, i, k: (i, k)),
                  pl.BlockSpec((tm, tk), lambda bb, i, k: (i, k)),
                  pl.BlockSpec((tk, FW), lambda bb, i, k: (kb0 + bb * nk + k, 0)),
                  pl.BlockSpec((tk, FW), lambda bb, i, k: (kb0 + bb * nk + k, 0))],
        out_specs=pl.BlockSpec((tm, FW), lambda bb, i, k: (bb * nm + i, 0)),
        out_shape=jax.ShapeDtypeStruct((B * S, FW), BF16),
        scratch_shapes=[pltpu.VMEM((tm, FW), F32)],
        compiler_params=_cp(("arbitrary",) * 3),
        name="fourier_mix",
    )(cs, ss, a, b)


def _merge_kernel(attn_ref, pool_ref, four_ref, gates_ref, x_ref, mod_ref, wa_ref, wp_ref, wf_ref,
                  wo_ref, gpost_ref, gpre_ref, x1_ref, h2_ref, *, D):
    mod = mod_ref[0]
    merged = (gates_ref[:, 0:D].astype(F32) * _dot(attn_ref[...], wa_ref[...])
              + gates_ref[:, D:2 * D].astype(F32) * _dot(pool_ref[...], wp_ref[...])
              + gates_ref[:, 2 * D:3 * D].astype(F32) * _dot(four_ref[...], wf_ref[...]))
    y = _dot(merged.astype(BF16), wo_ref[...])
    x1 = x_ref[...] + mod[:, 2 * D:3 * D] * _rms(y, gpost_ref[...])
    x1_ref[...] = x1
    h2 = _rms(x1, gpre_ref[...]) * (1.0 + mod[:, 4 * D:5 * D]) + mod[:, 3 * D:4 * D]
    h2_ref[...] = h2.astype(h2_ref.dtype)


def _merge(attn, pool, four, gates, x, mod, wa, wp, wf, wo, gpost, gpre, h2_dtype, cfg):
    T, D = x.shape
    TM = cfg["TM_MERGE"]
    row = lambda i: (i, 0)
    const = lambda i: (0, 0)
    return pl.pallas_call(
        functools.partial(_merge_kernel, D=D),
        grid=(T // TM,),
        in_specs=[pl.BlockSpec((TM, attn.shape[1]), row), pl.BlockSpec((TM, pool.shape[1]), row),
                  pl.BlockSpec((TM, four.shape[1]), row), pl.BlockSpec((TM, 3 * D), row),
                  pl.BlockSpec((TM, D), row),
                  pl.BlockSpec((1, 1, 6 * D), lambda i: (cfg["mod_row"](i, TM), 0, 0)),
                  _resident(wa.shape, const), _resident(wp.shape, const),
                  _resident(wf.shape, const), _resident(wo.shape, const),
                  pl.BlockSpec((1, D), const), pl.BlockSpec((1, D), const)],
        out_specs=[pl.BlockSpec((TM, D), row), pl.BlockSpec((TM, D), row)],
        out_shape=[jax.ShapeDtypeStruct((T, D), F32), jax.ShapeDtypeStruct((T, D), h2_dtype)],
        compiler_params=_cp(("arbitrary",)),
        name="merge_out_proj",
    )(attn, pool, four, gates, x, mod, wa, wp, wf, wo, gpost, gpre)


def _ffn_epilogue(y, x1_ref, mod_ref, gpost_ref, gnext_ref, modn_ref, x2_ref, hn_ref, D):
    mod = mod_ref[0]
    x2 = x1_ref[...] + mod[:, 5 * D:6 * D] * _rms(y, gpost_ref[...])
    x2_ref[...] = x2
    modn = modn_ref[0]
    hn = _rms(x2, gnext_ref[...]) * (1.0 + modn[:, D:2 * D]) + modn[:, 0:D]
    hn_ref[...] = hn.astype(BF16)


def _swiglu_mid(x, wg_ref, wu_ref):
    return (_silu(_dot(x, wg_ref[...])) * _dot(x, wu_ref[...])).astype(BF16)


def _ffn_kernel(h_ref, wg_ref, wu_ref, wd_ref, x1_ref, mod_ref, gpost_ref, gnext_ref, modn_ref,
                x2_ref, hn_ref, *, D):
    f = pl.program_id(1)
    part = _dot(_swiglu_mid(h_ref[...], wg_ref, wu_ref), wd_ref[...])

    @pl.when(f == 0)
    def _():
        x2_ref[...] = part

    @pl.when(f > 0)
    def _():
        x2_ref[...] += part

    @pl.when(f == pl.num_programs(1) - 1)
    def _():
        _ffn_epilogue(x2_ref[...], x1_ref, mod_ref, gpost_ref, gnext_ref, modn_ref, x2_ref, hn_ref, D)


def _dense_ffn(h, wg, wu, wd, layer, x1, mod, gpost, gnext, modn, cfg):
    T, D = h.shape
    DFF = wd.shape[1]
    TM = cfg["TM_FFN"]
    tf = cfg["TF"]
    row = lambda i, f: (i, 0)
    const = lambda i, f: (0, 0)
    modrow = lambda i, f: (cfg["mod_row"](i, TM), 0, 0)
    return pl.pallas_call(
        functools.partial(_ffn_kernel, D=D),
        grid=(T // TM, DFF // tf),
        in_specs=[pl.BlockSpec((TM, D), row),
                  pl.BlockSpec((None, D, tf), lambda i, f: (layer, 0, f)),
                  pl.BlockSpec((None, D, tf), lambda i, f: (layer, 0, f)),
                  pl.BlockSpec((None, tf, D), lambda i, f: (layer, f, 0)),
                  pl.BlockSpec((TM, D), row, pipeline_mode=pl.Buffered(1)),
                  pl.BlockSpec((1, 1, 6 * D), modrow),
                  pl.BlockSpec((1, D), const), pl.BlockSpec((1, D), const),
                  pl.BlockSpec((1, 1, 6 * D), modrow)],
        out_specs=[pl.BlockSpec((TM, D), row), pl.BlockSpec((TM, D), row)],
        out_shape=[jax.ShapeDtypeStruct((T, D), F32), jax.ShapeDtypeStruct((T, D), BF16)],
        compiler_params=_cp(("arbitrary", "arbitrary")),
        name="dense_ffn",
    )(h, wg, wu, wd, x1, mod, gpost, gnext, modn)


def _router_kernel(h_ref, wr_ref, eidx_ref, rank_ref, wts_ref, cnt_ref, base_sc, *, E, TMR):
    i = pl.program_id(0)

    @pl.when(i == 0)
    def _():
        base_sc[...] = jnp.zeros(base_sc.shape, F32)

    logits = lax.dot_general(wr_ref[...], h_ref[...].astype(F32), (((1,), (1,)), ((), ())),
                             preferred_element_type=F32, precision=lax.Precision.HIGHEST)
    e_iota = lax.broadcasted_iota(I32, (E, TMR), 0)
    m1 = jnp.max(logits, axis=0, keepdims=True)
    i1 = jnp.min(jnp.where(logits == m1, e_iota, E), axis=0, keepdims=True)
    rest = jnp.where(e_iota == i1, -jnp.inf, logits)
    m2 = jnp.max(rest, axis=0, keepdims=True)
    i2 = jnp.min(jnp.where(rest == m2, e_iota, E), axis=0, keepdims=True)
    ex = jnp.exp(m2 - m1)
    w1 = 1.0 / (1.0 + ex)
    w2 = ex / (1.0 + ex)

    oh1 = e_iota == i1
    oh2 = e_iota == i2
    oh = jnp.where(jnp.logical_or(oh1, oh2), 1.0, 0.0)
    tri = jnp.where(lax.broadcasted_iota(I32, (TMR, TMR), 0) < lax.broadcasted_iota(I32, (TMR, TMR), 1),
                    1.0, 0.0).astype(BF16)
    before = _dot(oh.astype(BF16), tri) + base_sc[...]
    r1 = jnp.sum(jnp.where(oh1, before, 0.0), axis=0, keepdims=True)
    r2 = jnp.sum(jnp.where(oh2, before, 0.0), axis=0, keepdims=True)
    base_sc[...] = base_sc[...] + jnp.sum(oh, axis=1, keepdims=True)

    eidx_ref[0:1, :] = i1
    eidx_ref[1:2, :] = i2
    rank_ref[0:1, :] = r1.astype(I32)
    rank_ref[1:2, :] = r2.astype(I32)
    wts_ref[0:1, :] = w1
    wts_ref[1:2, :] = w2
    cnt_ref[...] = jnp.broadcast_to(base_sc[...], cnt_ref.shape).astype(I32)


def _router(h, wr_t, cfg):
    T, D = h.shape
    E = wr_t.shape[0]
    TMR = cfg["TM"]
    col = lambda i: (0, i)
    return pl.pallas_call(
        functools.partial(_router_kernel, E=E, TMR=TMR),
        grid=(T // TMR,),
        in_specs=[pl.BlockSpec((TMR, D), lambda i: (i, 0)),
                  pl.BlockSpec((E, D), lambda i: (0, 0))],
        out_specs=[pl.BlockSpec((TOP_K, TMR), col), pl.BlockSpec((TOP_K, TMR), col),
                   pl.BlockSpec((TOP_K, TMR), col), pl.BlockSpec((E, LANES), lambda i: (0, 0))],
        out_shape=[jax.ShapeDtypeStruct((TOP_K, T), I32), jax.ShapeDtypeStruct((TOP_K, T), I32),
                   jax.ShapeDtypeStruct((TOP_K, T), F32), jax.ShapeDtypeStruct((E, LANES), I32)],
        scratch_shapes=[pltpu.VMEM((E, 1), F32)],
        compiler_params=_cp(("arbitrary",)),
        name="moe_router",
    )(h, wr_t)


ROW_DMA_UNROLL = 8


def _row_dma_burst(row_copy, n_rows):
    def start(rb, c):
        for u in range(ROW_DMA_UNROLL):
            for k in range(TOP_K):
                row_copy(rb * ROW_DMA_UNROLL + u, k).start(priority=(u * TOP_K + k) % 2)
        return c

    def wait(rb, c):
        for u in range(ROW_DMA_UNROLL):
            for k in range(TOP_K):
                row_copy(rb * ROW_DMA_UNROLL + u, k).wait()
        return c

    lax.fori_loop(0, n_rows // ROW_DMA_UNROLL, start, 0)
    lax.fori_loop(0, n_rows // ROW_DMA_UNROLL, wait, 0)


def _dispatch_kernel(dest_ref, h_ref, xs_in_ref, xs_ref, sem, *, T, TMD):
    del xs_in_ref
    base = pl.program_id(0) * TMD

    def row_copy(r, k):
        return pltpu.make_async_copy(h_ref.at[pl.ds(r, 1)],
                                     xs_ref.at[pl.ds(dest_ref[k * T + base + r], 1)], sem)

    _row_dma_burst(row_copy, TMD)


def _dispatch(dest_flat, h, P, cfg):
    T, D = h.shape
    TMD = cfg["TM"]
    xs0 = jnp.zeros((P, D), h.dtype)
    return pl.pallas_call(
        functools.partial(_dispatch_kernel, T=T, TMD=TMD),
        grid_spec=pltpu.PrefetchScalarGridSpec(
            num_scalar_prefetch=1,
            grid=(T // TMD,),
            in_specs=[pl.BlockSpec((TMD, D), lambda i, dest: (i, 0)),
                      pl.BlockSpec(memory_space=pl.ANY)],
            out_specs=pl.BlockSpec(memory_space=pl.ANY),
            scratch_shapes=[pltpu.SemaphoreType.DMA(())]),
        out_shape=jax.ShapeDtypeStruct((P, D), h.dtype),
        input_output_aliases={2: 0},
        compiler_params=pltpu.CompilerParams(dimension_semantics=("arbitrary",),
                                             has_side_effects=True),
        name="moe_dispatch",
    )(dest_flat, h, xs0)


def _moe_kernel(te_ref, nu_ref, xs_ref, wg_ref, wu_ref, wd_ref, y_ref, acc_sc, xb_sc):
    j = pl.program_id(0)
    f = pl.program_id(1)
    used = j < nu_ref[0]
    last = f == pl.num_programs(1) - 1

    @pl.when(jnp.logical_and(used, f == 0))
    def _():
        acc_sc[...] = jnp.zeros(acc_sc.shape, F32)
        xb_sc[...] = xs_ref[...].astype(BF16)

    @pl.when(used)
    def _():
        acc_sc[...] += _dot(_swiglu_mid(xb_sc[...], wg_ref, wu_ref), wd_ref[...])

    @pl.when(jnp.logical_and(used, last))
    def _():
        y_ref[...] = acc_sc[...]

    @pl.when(jnp.logical_and(jnp.logical_not(used), last))
    def _():
        y_ref[...] = jnp.zeros(y_ref.shape, F32)


def _moe_ffn(tile_expert, n_used, xs, wg, wu, wd, layer, cfg):
    P, D = xs.shape
    DFE = wd.shape[2]
    TME, tf = cfg["TME"], cfg["TFE"]
    nf = DFE // tf

    def jj(j, nu):
        return jnp.minimum(j, nu[0] - 1)

    def ff(j, f, nu):
        return jnp.where(j < nu[0], f, nf - 1)

    return pl.pallas_call(
        _moe_kernel,
        grid_spec=pltpu.PrefetchScalarGridSpec(
            num_scalar_prefetch=2,
            grid=(P // TME, nf),
            in_specs=[pl.BlockSpec((TME, D), lambda j, f, te, nu: (jj(j, nu), 0)),
                      pl.BlockSpec((None, None, D, tf),
                                   lambda j, f, te, nu: (layer, te[jj(j, nu)], 0, ff(j, f, nu))),
                      pl.BlockSpec((None, None, D, tf),
                                   lambda j, f, te, nu: (layer, te[jj(j, nu)], 0, ff(j, f, nu))),
                      pl.BlockSpec((None, None, tf, D),
                                   lambda j, f, te, nu: (layer, te[jj(j, nu)], ff(j, f, nu), 0))],
            out_specs=pl.BlockSpec((TME, D), lambda j, f, te, nu: (j, 0)),
            scratch_shapes=[pltpu.VMEM((TME, D), F32), pltpu.VMEM((TME, D), BF16)]),
        out_shape=jax.ShapeDtypeStruct((P, D), F32),
        compiler_params=_cp(("arbitrary", "arbitrary")),
        name="moe_expert_ffn",
    )(tile_expert, n_used, xs, wg, wu, wd)


def _combine_kernel(dest_ref, y_ref, wts_ref, x1_ref, mod_ref, gpost_ref, gnext_ref, modn_ref,
                    x2_ref, hn_ref, ybuf, sem, *, T, TMC, D):
    base = pl.program_id(0) * TMC

    def row_copy(r, k):
        return pltpu.make_async_copy(y_ref.at[pl.ds(dest_ref[k * T + base + r], 1)],
                                     ybuf.at[k, pl.ds(r, 1)], sem)

    _row_dma_burst(row_copy, TMC)
    w = wts_ref[...]
    y = w[:, 0:1] * ybuf[0] + w[:, 1:2] * ybuf[1]
    _ffn_epilogue(y, x1_ref, mod_ref, gpost_ref, gnext_ref, modn_ref, x2_ref, hn_ref, D)


def _combine(dest_flat, y, wts, x1, mod, gpost, gnext, modn, cfg):
    T, D = x1.shape
    TMC = cfg["TM_MERGE"]
    row = lambda i, d: (i, 0)
    const = lambda i, d: (0, 0)
    modrow = lambda i, d: (cfg["mod_row"](i, TMC), 0, 0)
    return pl.pallas_call(
        functools.partial(_combine_kernel, T=T, TMC=TMC, D=D),
        grid_spec=pltpu.PrefetchScalarGridSpec(
            num_scalar_prefetch=1,
            grid=(T // TMC,),
            in_specs=[pl.BlockSpec(memory_space=pl.ANY),
                      pl.BlockSpec((TMC, TOP_K), row),
                      pl.BlockSpec((TMC, D), row),
                      pl.BlockSpec((1, 1, 6 * D), modrow),
                      pl.BlockSpec((1, D), const), pl.BlockSpec((1, D), const),
                      pl.BlockSpec((1, 1, 6 * D), modrow)],
            out_specs=[pl.BlockSpec((TMC, D), row), pl.BlockSpec((TMC, D), row)],
            scratch_shapes=[pltpu.VMEM((TOP_K, TMC, D), F32), pltpu.SemaphoreType.DMA(())]),
        out_shape=[jax.ShapeDtypeStruct((T, D), F32), jax.ShapeDtypeStruct((T, D), BF16)],
        compiler_params=_cp(("arbitrary",)),
        name="moe_combine",
    )(dest_flat, y, wts, x1, mod, gpost, gnext, modn)


def _rope_tables(Ss, TM):
    n_freq = LANES // 4
    t = jnp.arange(Ss, dtype=I32)
    rows = (t // GRID_W).astype(F32)
    cols = (t % GRID_W).astype(F32)
    inv_freq = jnp.power(ROPE_THETA, -jnp.arange(n_freq, dtype=F32) / n_freq)
    ar = rows[:, None] * inv_freq
    ac = cols[:, None] * inv_freq
    cos = jnp.concatenate([jnp.cos(ar), jnp.cos(ar), jnp.cos(ac), jnp.cos(ac)], axis=-1)
    sin = jnp.concatenate([-jnp.sin(ar), jnp.sin(ar), -jnp.sin(ac), jnp.sin(ac)], axis=-1)
    cos = jnp.concatenate([jnp.ones((TM, LANES), F32), cos], axis=0)
    sin = jnp.concatenate([jnp.zeros((TM, LANES), F32), sin], axis=0)
    return cos, sin


def _dft_tables(n):
    j = jnp.arange(n, dtype=I32)
    ph = (j[:, None] * j[None, :]) % n
    ang = ph.astype(F32) * (2.0 * math.pi / n)
    return jnp.cos(ang).astype(BF16), jnp.sin(ang).astype(BF16)


def kernel(x_prompt, x_sample, cache_k, cache_v, c, c_ctx, w_mod, b_mod, g_pre_mix, g_post_mix,
           g_pre_ffn, g_post_ffn, w_in, g_q, g_k, w_pool_map, pool_scale, w_attn_o, w_pool_o,
           w_four_o, w_out, w_ffn_gate, w_ffn_up, w_ffn_down, w_router, w_exp_gate, w_exp_up,
           w_exp_down):
    Bp, Sp, D = x_prompt.shape
    Bs, Ss, _ = x_sample.shape
    L = w_mod.shape[0]
    PAST, KV = cache_k.shape[2], cache_k.shape[3]
    AW, PW, FW = w_attn_o.shape[1], w_pool_o.shape[1], w_four_o.shape[1]
    NH = AW // LANES
    G = NH // KV
    KW = KV * LANES
    E = w_router.shape[2]
    NP, NS = Bp * Sp, Bs * Ss
    T = NP + NS
    TM = min(512, math.gcd(NP, Ss))
    assert cache_k.shape[4] == LANES and NP % TM == 0 and Ss % TM == 0 and NP % Ss == 0
    assert Ss % Sp == 0 and Sp & (Sp - 1) == 0 and Ss & (Ss - 1) == 0 and Ss % GRID_W == 0
    NPT, TPS = NP // TM, Ss // TM
    TM_BIG = min(1024, math.gcd(NP, Ss))
    TME = TM
    P = (-(-(T * TOP_K) // TME) + E) * TME

    pick_tile = lambda n, cands: next(t for t in cands if n % t == 0)
    cfg = dict(
        TM=TM, TM_MERGE=min(TM, 256), TME=TME, TM_FFN=TM, TM_GATES=TM_BIG,
        TF=pick_tile(w_ffn_gate.shape[2], (512, 256, 128)),
        TFE=pick_tile(w_exp_gate.shape[3], (1024, 512, 256, 128)),
        NH=NH, KV=KV, PW=PW, FW=FW, NP=NP, Sp=Sp, Ss=Ss,
        mod_row=lambda i, tm=TM: jnp.where(i * tm < NP, 0, 1 + (i * tm - NP) // Ss),
        rope_blk=lambda i: jnp.where(i < NPT, 0, 1 + (i - NPT) % TPS),
    )

    x = jnp.concatenate([x_prompt.reshape(NP, D), x_sample.reshape(NS, D)], axis=0)
    R = -(-(1 + Bs) // 8) * 8
    cond = jnp.zeros((R, D), F32).at[0].set(c_ctx).at[1:1 + Bs].set(c)
    mod_all = _modulation(cond, w_mod, b_mod).reshape(L, R, 1, 6 * D)

    o1, o2, o3, o4 = AW, AW + 2 * KW, AW + 2 * KW + PW, AW + 2 * KW + PW + FW
    w_in_b = w_in.astype(BF16)
    cos_tab, sin_tab = _rope_tables(Ss, TM)
    cc, sc = _dft_tables(LANES)
    cs_chan = jnp.concatenate([cc, sc], axis=1)
    dft_p = _dft_tables(Sp)
    dft_s = _dft_tables(Ss)
    ck = cache_k.astype(BF16).reshape(Bs, L, PAST, KW)
    cv = cache_v.astype(BF16).reshape(Bs, L, PAST, KW)
    row1 = lambda a: a.reshape(1, -1)
    dense_w = [w.astype(BF16) for w in (w_ffn_gate, w_ffn_up, w_ffn_down)]
    expert_w = [w.astype(BF16) for w in (w_exp_gate, w_exp_up, w_exp_down)]

    h = _norm_mod(x, row1(g_pre_mix[0]), mod_all[0], cfg)
    new_k, new_v = [], []
    for l in range(L):
        mod = mod_all[l]
        q, kb, vb, kf, vf = _qkv(h, w_in_b[l, :, :o2], row1(g_q[l]), row1(g_k[l]), cos_tab, sin_tab, cfg)
        u_pool, fa, fb = _pool_fourier_proj(h, w_in_b[l, :, o2:o4], cs_chan, cfg)
        gates = _gates(h, w_in_b[l, :, o4:], cfg)
        new_k.append(kf[:NP].reshape(Bp, Sp, KV, LANES))
        new_v.append(vf[:NP].reshape(Bp, Sp, KV, LANES))

        attn_p = _attention(q, kb, vb, q_row0=0, B=Bp, S=Sp, Tk=Sp, KV=KV, G=G)
        k_s = jnp.concatenate([ck[:, l], kb[NP:].reshape(Bs, Ss, KW)], axis=1).reshape(-1, KW)
        v_s = jnp.concatenate([cv[:, l], vb[NP:].reshape(Bs, Ss, KW)], axis=1).reshape(-1, KW)
        attn_s = _attention(q, k_s, v_s, q_row0=NP, B=Bs, S=Ss, Tk=PAST + Ss, KV=KV, G=G)
        attn = jnp.concatenate([attn_p, attn_s], axis=0)

        pool = _pool(u_pool, w_pool_map[l].astype(BF16), row1(pool_scale[l]), cfg)
        four = jnp.concatenate([
            _fourier(fa, fb, *dft_p, row0=0, B=Bp, S=Sp),
            _fourier(fa, fb, *dft_s, row0=NP, B=Bs, S=Ss)], axis=0)

        x1, h2 = _merge(attn, pool, four, gates, x, mod, w_attn_o[l].astype(BF16),
                        w_pool_o[l].astype(BF16), w_four_o[l].astype(BF16), w_out[l].astype(BF16),
                        row1(g_post_mix[l]), row1(g_pre_ffn[l]), BF16 if l % 2 == 0 else F32, cfg)

        ln = min(l + 1, L - 1)
        gnext, modn = row1(g_pre_mix[ln]), mod_all[ln]
        i = l // 2
        if l % 2 == 0:
            x, h = _dense_ffn(h2, *dense_w, i, x1, mod, row1(g_post_ffn[l]), gnext, modn, cfg)
        else:
            eidx, rank, wts, cnt = _router(h2, w_router[i].T, cfg)
            counts = cnt[:, 0]
            padded = (counts + TME - 1) // TME * TME
            pad_end = jnp.cumsum(padded)
            pad_start = pad_end - padded
            start_of = sum(jnp.where(eidx == e, pad_start[e], 0) for e in range(E))
            dest = (start_of + rank).reshape(-1).astype(I32)
            n_used = (pad_end[-1] // TME).astype(I32).reshape(1)
            tile_start = jnp.arange(P // TME, dtype=I32) * TME
            tile_expert = jnp.minimum(
                jnp.sum(tile_start[:, None] >= pad_end[None, :], axis=1), E - 1).astype(I32)
            xs = _dispatch(dest, h2, P, cfg)
            y = _moe_ffn(tile_expert, n_used, xs, *expert_w, i, cfg)
            x, h = _combine(dest, y, wts.T, x1, mod, row1(g_post_ffn[l]), gnext, modn, cfg)

    y_prompt = x[:NP].reshape(Bp, Sp, D)
    y_sample = x[NP:].reshape(Bs, Ss, D)
    return (y_prompt, y_sample, jnp.stack(new_k, axis=1), jnp.stack(new_v, axis=1))
```

```python
import functools
import math

import jax
import jax.numpy as jnp
from jax import lax
from jax.experimental import pallas as pl
from jax.experimental.pallas import tpu as pltpu

F32 = jnp.float32
BF16 = jnp.bfloat16
I32 = jnp.int32

LANES = 128
GRID_W = 64
ROPE_THETA = 10000.0
EPS = 1e-6
TOP_K = 2
POOL_WINDOWS = (2, 4, 8, 16)
VMEM_LIMIT = 56 * 1024 * 1024


def _cp(sem, vmem=VMEM_LIMIT):
    return pltpu.CompilerParams(dimension_semantics=sem, vmem_limit_bytes=vmem)


def _rms(x, g):
    return x * lax.rsqrt(jnp.mean(x * x, axis=-1, keepdims=True) + EPS) * g


def _silu(x):
    return x / (1.0 + jnp.exp(-x))


def _dot(a, b):
    return jnp.dot(a, b, preferred_element_type=F32)


def _resident(shape, index_map):
    return pl.BlockSpec(shape, index_map, pipeline_mode=pl.Buffered(1))


def _mod_kernel(c_ref, w_ref, b_ref, o_ref):
    s = _silu(c_ref[...])
    o_ref[0] = jnp.dot(s, w_ref[0], preferred_element_type=F32,
                       precision=lax.Precision.HIGHEST) + b_ref[0]


def _modulation(cond, w_mod, b_mod):
    L, D, N = w_mod.shape
    R = cond.shape[0]
    tn = 1024
    return pl.pallas_call(
        _mod_kernel,
        grid=(L, N // tn),
        in_specs=[pl.BlockSpec((R, D), lambda l, j: (0, 0)),
                  pl.BlockSpec((1, D, tn), lambda l, j: (l, 0, j)),
                  pl.BlockSpec((1, 1, tn), lambda l, j: (l, 0, j))],
        out_specs=pl.BlockSpec((1, R, tn), lambda l, j: (l, 0, j)),
        out_shape=jax.ShapeDtypeStruct((L, R, N), F32),
        compiler_params=_cp(("arbitrary", "arbitrary")),
        name="modulation",
    )(cond, w_mod, b_mod.reshape(L, 1, N))


def _norm_mod_kernel(x_ref, g_ref, mod_ref, h_ref, *, D):
    mod = mod_ref[0]
    y = _rms(x_ref[...], g_ref[...])
    h_ref[...] = (y * (1.0 + mod[:, D:2 * D]) + mod[:, 0:D]).astype(BF16)


def _norm_mod(x, g, mod, cfg):
    T, D = x.shape
    TM = cfg["TM"]
    return pl.pallas_call(
        functools.partial(_norm_mod_kernel, D=D),
        grid=(T // TM,),
        in_specs=[pl.BlockSpec((TM, D), lambda i: (i, 0)),
                  pl.BlockSpec((1, D), lambda i: (0, 0)),
                  pl.BlockSpec((1, 1, 6 * D), lambda i: (cfg["mod_row"](i), 0, 0))],
        out_specs=pl.BlockSpec((TM, D), lambda i: (i, 0)),
        out_shape=jax.ShapeDtypeStruct((T, D), BF16),
        compiler_params=_cp(("arbitrary",)),
        name="norm_mod",
    )(x, g, mod)


def _qkv_kernel(h_ref, w_ref, gq_ref, gk_ref, cos_ref, sin_ref,
                q_ref, kb_ref, vb_ref, kf_ref, vf_ref, *, NH, KV, scale):
    acc = _dot(h_ref[...], w_ref[...])
    cos = cos_ref[...]
    sin = sin_ref[...]
    lane = lax.broadcasted_iota(I32, cos.shape, 1)
    first_half = (lane % (LANES // 2)) < (LANES // 4)

    def rope(y):
        partner = jnp.where(first_half, pltpu.roll(y, LANES - LANES // 4, axis=1),
                            pltpu.roll(y, LANES // 4, axis=1))
        return y * cos + partner * sin

    for hd in range(NH):
        z = acc[:, hd * LANES:(hd + 1) * LANES]
        q_ref[:, hd * LANES:(hd + 1) * LANES] = (rope(_rms(z, gq_ref[...])) * scale).astype(BF16)
    for hd in range(KV):
        z = acc[:, (NH + hd) * LANES:(NH + hd + 1) * LANES]
        y = _rms(z, gk_ref[...])
        kf_ref[:, hd * LANES:(hd + 1) * LANES] = y
        kb_ref[:, hd * LANES:(hd + 1) * LANES] = rope(y).astype(BF16)
    v = acc[:, (NH + KV) * LANES:(NH + 2 * KV) * LANES]
    vf_ref[...] = v
    vb_ref[...] = v.astype(BF16)


def _qkv(h, w, gq, gk, cos_tab, sin_tab, cfg):
    T, D = h.shape
    TM, NH, KV = cfg["TM"], cfg["NH"], cfg["KV"]
    AW, KW = NH * LANES, KV * LANES
    row = lambda i: (i, 0)
    return pl.pallas_call(
        functools.partial(_qkv_kernel, NH=NH, KV=KV, scale=float(LANES) ** -0.5 * math.log2(math.e)),
        grid=(T // TM,),
        in_specs=[pl.BlockSpec((TM, D), row),
                  _resident((D, AW + 2 * KW), lambda i: (0, 0)),
                  pl.BlockSpec((1, LANES), lambda i: (0, 0)),
                  pl.BlockSpec((1, LANES), lambda i: (0, 0)),
                  pl.BlockSpec((TM, LANES), lambda i: (cfg["rope_blk"](i), 0)),
                  pl.BlockSpec((TM, LANES), lambda i: (cfg["rope_blk"](i), 0))],
        out_specs=[pl.BlockSpec((TM, AW), row), pl.BlockSpec((TM, KW), row),
                   pl.BlockSpec((TM, KW), row), pl.BlockSpec((TM, KW), row),
                   pl.BlockSpec((TM, KW), row)],
        out_shape=[jax.ShapeDtypeStruct((T, AW), BF16), jax.ShapeDtypeStruct((T, KW), BF16),
                   jax.ShapeDtypeStruct((T, KW), BF16), jax.ShapeDtypeStruct((T, KW), F32),
                   jax.ShapeDtypeStruct((T, KW), F32)],
        compiler_params=_cp(("arbitrary",)),
        name="qkv_proj",
    )(h, w, gq, gk, cos_tab, sin_tab)


def _pf_kernel(h_ref, w_ref, cs_ref, up_ref, a_ref, b_ref, *, PW, NG):
    acc = _dot(h_ref[...], w_ref[...])
    up_ref[...] = acc[:, :PW]
    for g in range(NG):
        u = acc[:, PW + g * LANES:PW + (g + 1) * LANES].astype(BF16)
        ab = _dot(u, cs_ref[...])
        a_ref[:, g * LANES:(g + 1) * LANES] = ab[:, :LANES].astype(BF16)
        b_ref[:, g * LANES:(g + 1) * LANES] = ab[:, LANES:].astype(BF16)


def _pool_fourier_proj(h, w, cs, cfg):
    T, D = h.shape
    TM, PW, FW = cfg["TM"], cfg["PW"], cfg["FW"]
    row = lambda i: (i, 0)
    return pl.pallas_call(
        functools.partial(_pf_kernel, PW=PW, NG=FW // LANES),
        grid=(T // TM,),
        in_specs=[pl.BlockSpec((TM, D), row),
                  _resident((D, PW + FW), lambda i: (0, 0)),
                  pl.BlockSpec((LANES, 2 * LANES), lambda i: (0, 0))],
        out_specs=[pl.BlockSpec((TM, PW), row), pl.BlockSpec((TM, FW), row),
                   pl.BlockSpec((TM, FW), row)],
        out_shape=[jax.ShapeDtypeStruct((T, PW), F32), jax.ShapeDtypeStruct((T, FW), BF16),
                   jax.ShapeDtypeStruct((T, FW), BF16)],
        compiler_params=_cp(("arbitrary",)),
        name="pool_fourier_proj",
    )(h, w, cs)


def _gates_kernel(h_ref, w_ref, o_ref):
    z = _dot(h_ref[...], w_ref[...])
    o_ref[...] = (1.0 / (1.0 + jnp.exp(-z))).astype(BF16)


def _gates(h, w, cfg):
    T, D = h.shape
    N = w.shape[1]
    TM = cfg["TM_GATES"]
    tn = 1024
    return pl.pallas_call(
        _gates_kernel,
        grid=(N // tn, T // TM),
        in_specs=[pl.BlockSpec((TM, D), lambda j, i: (i, 0)),
                  pl.BlockSpec((D, tn), lambda j, i: (0, j))],
        out_specs=pl.BlockSpec((TM, tn), lambda j, i: (i, j)),
        out_shape=jax.ShapeDtypeStruct((T, N), BF16),
        compiler_params=_cp(("arbitrary", "arbitrary")),
        name="gates_proj",
    )(h, w)


def _attn_kernel(q_ref, k_ref, v_ref, o_ref, *, G, tq, tk, nk):
    q = jnp.concatenate([q_ref[:, g * LANES:(g + 1) * LANES] for g in range(G)], axis=0)
    rows = G * tq
    nt = tk // LANES
    m = jnp.full((rows, LANES), -jnp.inf, F32)
    l = jnp.zeros((rows, LANES), F32)
    acc = jnp.zeros((rows, LANES), F32)
    for c in range(nk):
        k = k_ref[c * tk:(c + 1) * tk, :]
        v = v_ref[c * tk:(c + 1) * tk, :]
        s = lax.dot_general(q, k, (((1,), (1,)), ((), ())), preferred_element_type=F32)
        m_new = jnp.maximum(m, jnp.max(s, axis=-1, keepdims=True))
        alpha = jnp.exp2(m - m_new)
        p = [jnp.exp2(s[:, t * LANES:(t + 1) * LANES] - m_new) for t in range(nt)]
        l = alpha * l + functools.reduce(lambda a, b: a + b, p)
        pb = jnp.concatenate([pt.astype(BF16) for pt in p], axis=1)
        acc = alpha * acc + _dot(pb, v)
        m = m_new
    o = acc / jnp.sum(l, axis=-1, keepdims=True)
    for g in range(G):
        o_ref[:, g * LANES:(g + 1) * LANES] = o[g * tq:(g + 1) * tq].astype(BF16)


def _attention(q, k, v, *, q_row0, B, S, Tk, KV, G):
    tq = min(256, S)
    tk = next(t for t in (512, 256, 128) if Tk % t == 0)
    nq, nk = S // tq, Tk // tk
    qb0 = q_row0 // tq
    return pl.pallas_call(
        functools.partial(_attn_kernel, G=G, tq=tq, tk=tk, nk=nk),
        grid=(B, KV, nq),
        in_specs=[pl.BlockSpec((tq, G * LANES), lambda b, h, qi: (qb0 + b * nq + qi, h)),
                  pl.BlockSpec((Tk, LANES), lambda b, h, qi: (b, h)),
                  pl.BlockSpec((Tk, LANES), lambda b, h, qi: (b, h))],
        out_specs=pl.BlockSpec((tq, G * LANES), lambda b, h, qi: (b * nq + qi, h)),
        out_shape=jax.ShapeDtypeStruct((B * S, KV * G * LANES), BF16),
        compiler_params=_cp(("arbitrary",) * 3),
        name="attention",
    )(q, k, v)


def _pool_kernel(u_ref, wmap_ref, scale_ref, o_ref, tot_sc, *, R, n_prompt_blocks, Sp, Ss):
    blk = pl.program_id(0)
    g = pl.program_id(1)
    S = jnp.where(blk < n_prompt_blocks, Sp, Ss)
    half = jnp.left_shift(1, g)
    pos = lax.broadcasted_iota(I32, (R, LANES), 0) & (S - 1)

    def add_offsets(offsets):
        u = u_ref[...]
        tot = tot_sc[...]
        for k in offsets:
            valid = jnp.logical_and(pos + k >= 0, pos + k < S)
            tot = tot + jnp.where(valid, pltpu.roll(u, (-k) % R, axis=0), 0.0)
        tot_sc[...] = tot

    tot_sc[...] = u_ref[...]
    add_offsets([-1])
    for lvl in range(1, len(POOL_WINDOWS)):
        h0, h1 = POOL_WINDOWS[lvl - 1] // 2, POOL_WINDOWS[lvl] // 2

        @pl.when(g >= lvl)
        def _():
            add_offsets(list(range(-h1, -h0)) + list(range(h0, h1)))

    cnt = (jnp.minimum(pos + half, S) - jnp.maximum(pos - half, 0)).astype(F32)
    pooled = tot_sc[...] / cnt - u_ref[...]
    mixed = _dot(pooled.astype(BF16), wmap_ref[0]) * scale_ref[...]
    o_ref[...] = mixed.astype(BF16)


def _pool(u, wmap, scale, cfg):
    T, PW = u.shape
    R, NP, Sp, Ss = cfg["Ss"], cfg["NP"], cfg["Sp"], cfg["Ss"]
    return pl.pallas_call(
        functools.partial(_pool_kernel, R=R, n_prompt_blocks=NP // R, Sp=Sp, Ss=Ss),
        grid=(T // R, PW // LANES),
        in_specs=[pl.BlockSpec((R, LANES), lambda b, g: (b, g)),
                  pl.BlockSpec((1, LANES, LANES), lambda b, g: (g, 0, 0)),
                  pl.BlockSpec((1, LANES), lambda b, g: (0, g))],
        out_specs=pl.BlockSpec((R, LANES), lambda b, g: (b, g)),
        out_shape=jax.ShapeDtypeStruct((T, PW), BF16),
        scratch_shapes=[pltpu.VMEM((R, LANES), F32)],
        compiler_params=_cp(("arbitrary", "arbitrary")),
        name="multiscale_pool",
    )(u, wmap, scale)


def _fourier_kernel(cs_ref, ss_ref, a_ref, b_ref, o_ref, acc_sc, *, scale):
    k = pl.program_id(2)

    @pl.when(k == 0)
    def _():
        acc_sc[...] = jnp.zeros(acc_sc.shape, F32)

    acc_sc[...] += _dot(cs_ref[...], a_ref[...]) - _dot(ss_ref[...], b_ref[...])

    @pl.when(k == pl.num_programs(2) - 1)
    def _():
        o_ref[...] = (acc_sc[...] * scale).astype(BF16)


def _fourier(a, b, cs, ss, *, row0, B, S):
    FW = a.shape[1]
    tm = min(1024, S)
    tk = min(1024, S)
    nm, nk = S // tm, S // tk
    kb0 = row0 // tk
    return pl.pallas_call(
        functools.partial(_fourier_kernel, scale=float(S * LANES) ** -0.5),
        grid=(B, nm, nk),
        in_specs=[pl.BlockSpec((tm, tk), lambda bb, i, k: (i, k)),
                  pl.BlockSpec((tm, tk), lambda bb, i, k: (i, k)),
                  pl.BlockSpec((tk, FW), lambda bb, i, k: (kb0 + bb * nk + k, 0)),
                  pl.BlockSpec((tk, FW), lambda bb, i, k: (kb0 + bb * nk + k, 0))],
        out_specs=pl.BlockSpec((tm, FW), lambda bb, i, k: (bb * nm + i, 0)),
        out_shape=jax.ShapeDtypeStruct((B * S, FW), BF16),
        scratch_shapes=[pltpu.VMEM((tm, FW), F32)],
        compiler_params=_cp(("arbitrary",) * 3),
        name="fourier_mix",
    )(cs, ss, a, b)


def _merge_kernel(attn_p_ref, attn_s_ref, pool_ref, four_p_ref, four_s_ref, gates_ref, x_ref, mod_ref,
                  wa_ref, wp_ref, wf_ref, wo_ref, gpost_ref, gpre_ref, x1_ref, h2_ref, *, D, n_prompt_tiles):
    mod = mod_ref[0]
    is_prompt = pl.program_id(0) < n_prompt_tiles
    attn = jnp.where(is_prompt, attn_p_ref[...], attn_s_ref[...])
    four = jnp.where(is_prompt, four_p_ref[...], four_s_ref[...])
    merged = (gates_ref[:, 0:D].astype(F32) * _dot(attn, wa_ref[...])
              + gates_ref[:, D:2 * D].astype(F32) * _dot(pool_ref[...], wp_ref[...])
              + gates_ref[:, 2 * D:3 * D].astype(F32) * _dot(four, wf_ref[...]))
    y = _dot(merged.astype(BF16), wo_ref[...])
    x1 = x_ref[...] + mod[:, 2 * D:3 * D] * _rms(y, gpost_ref[...])
    x1_ref[...] = x1
    h2 = _rms(x1, gpre_ref[...]) * (1.0 + mod[:, 4 * D:5 * D]) + mod[:, 3 * D:4 * D]
    h2_ref[...] = h2.astype(h2_ref.dtype)


def _merge(attn_p, attn_s, pool, four_p, four_s, gates, x, mod, wa, wp, wf, wo, gpost, gpre, h2_dtype, cfg):
    T, D = x.shape
    TM = cfg["TM_MERGE"]
    npt = cfg["NP"] // TM
    row = lambda i: (i, 0)
    const = lambda i: (0, 0)
    prow = lambda i: (jnp.minimum(i, npt - 1), 0)
    srow = lambda i: (jnp.maximum(i - npt, 0), 0)
    AW, FW = attn_p.shape[1], four_p.shape[1]
    return pl.pallas_call(
        functools.partial(_merge_kernel, D=D, n_prompt_tiles=npt),
        grid=(T // TM,),
        in_specs=[pl.BlockSpec((TM, AW), prow), pl.BlockSpec((TM, AW), srow),
                  pl.BlockSpec((TM, pool.shape[1]), row),
                  pl.BlockSpec((TM, FW), prow), pl.BlockSpec((TM, FW), srow),
                  pl.BlockSpec((TM, 3 * D), row),
                  pl.BlockSpec((TM, D), row),
                  pl.BlockSpec((1, 1, 6 * D), lambda i: (cfg["mod_row"](i, TM), 0, 0)),
                  _resident(wa.shape, const), _resident(wp.shape, const),
                  _resident(wf.shape, const), _resident(wo.shape, const),
                  pl.BlockSpec((1, D), const), pl.BlockSpec((1, D), const)],
        out_specs=[pl.BlockSpec((TM, D), row), pl.BlockSpec((TM, D), row)],
        out_shape=[jax.ShapeDtypeStruct((T, D), F32), jax.ShapeDtypeStruct((T, D), h2_dtype)],
        compiler_params=_cp(("arbitrary",)),
        name="merge_out_proj",
    )(attn_p, attn_s, pool, four_p, four_s, gates, x, mod, wa, wp, wf, wo, gpost, gpre)


def _ffn_epilogue(y, x1_ref, mod_ref, gpost_ref, gnext_ref, modn_ref, x2_ref, hn_ref, D):
    mod = mod_ref[0]
    x2 = x1_ref[...] + mod[:, 5 * D:6 * D] * _rms(y, gpost_ref[...])
    x2_ref[...] = x2
    modn = modn_ref[0]
    hn = _rms(x2, gnext_ref[...]) * (1.0 + modn[:, D:2 * D]) + modn[:, 0:D]
    hn_ref[...] = hn.astype(BF16)


def _swiglu_mid(x, wg_ref, wu_ref):
    return (_silu(_dot(x, wg_ref[...])) * _dot(x, wu_ref[...])).astype(BF16)


def _ffn_kernel(h_ref, wg_ref, wu_ref, wd_ref, x1_ref, mod_ref, gpost_ref, gnext_ref, modn_ref,
                x2_ref, hn_ref, acc_sc, *, D):
    f = pl.program_id(1)

    @pl.when(f == 0)
    def _():
        acc_sc[...] = jnp.zeros(acc_sc.shape, F32)

    acc_sc[...] += _dot(_swiglu_mid(h_ref[...], wg_ref, wu_ref), wd_ref[...])

    @pl.when(f == pl.num_programs(1) - 1)
    def _():
        _ffn_epilogue(acc_sc[...], x1_ref, mod_ref, gpost_ref, gnext_ref, modn_ref, x2_ref, hn_ref, D)


def _dense_ffn(h, wg, wu, wd, layer, x1, mod, gpost, gnext, modn, cfg):
    T, D = h.shape
    DFF = wd.shape[1]
    TM = cfg["TM_FFN"]
    tf = cfg["TF"]
    row = lambda i, f: (i, 0)
    const = lambda i, f: (0, 0)
    modrow = lambda i, f: (cfg["mod_row"](i, TM), 0, 0)
    return pl.pallas_call(
        functools.partial(_ffn_kernel, D=D),
        grid=(T // TM, DFF // tf),
        in_specs=[pl.BlockSpec((TM, D), row),
                  pl.BlockSpec((None, D, tf), lambda i, f: (layer, 0, f)),
                  pl.BlockSpec((None, D, tf), lambda i, f: (layer, 0, f)),
                  pl.BlockSpec((None, tf, D), lambda i, f: (layer, f, 0)),
                  pl.BlockSpec((TM, D), row),
                  pl.BlockSpec((1, 1, 6 * D), modrow),
                  pl.BlockSpec((1, D), const), pl.BlockSpec((1, D), const),
                  pl.BlockSpec((1, 1, 6 * D), modrow)],
        out_specs=[pl.BlockSpec((TM, D), row), pl.BlockSpec((TM, D), row)],
        out_shape=[jax.ShapeDtypeStruct((T, D), F32), jax.ShapeDtypeStruct((T, D), BF16)],
        scratch_shapes=[pltpu.VMEM((TM, D), F32)],
        compiler_params=_cp(("arbitrary", "arbitrary")),
        name="dense_ffn",
    )(h, wg, wu, wd, x1, mod, gpost, gnext, modn)


def _router_kernel(h_ref, wr_ref, eidx_ref, rank_ref, wts_ref, cnt_ref, base_sc, *, E, TMR):
    i = pl.program_id(0)

    @pl.when(i == 0)
    def _():
        base_sc[...] = jnp.zeros(base_sc.shape, F32)

    logits = lax.dot_general(wr_ref[...], h_ref[...].astype(F32), (((1,), (1,)), ((), ())),
                             preferred_element_type=F32, precision=lax.Precision.HIGHEST)
    e_iota = lax.broadcasted_iota(I32, (E, TMR), 0)
    m1 = jnp.max(logits, axis=0, keepdims=True)
    i1 = jnp.min(jnp.where(logits == m1, e_iota, E), axis=0, keepdims=True)
    rest = jnp.where(e_iota == i1, -jnp.inf, logits)
    m2 = jnp.max(rest, axis=0, keepdims=True)
    i2 = jnp.min(jnp.where(rest == m2, e_iota, E), axis=0, keepdims=True)
    ex = jnp.exp(m2 - m1)
    w1 = 1.0 / (1.0 + ex)
    w2 = ex / (1.0 + ex)

    oh1 = e_iota == i1
    oh2 = e_iota == i2
    oh = jnp.where(jnp.logical_or(oh1, oh2), 1.0, 0.0)
    tri = jnp.where(lax.broadcasted_iota(I32, (TMR, TMR), 0) < lax.broadcasted_iota(I32, (TMR, TMR), 1),
                    1.0, 0.0).astype(BF16)
    before = _dot(oh.astype(BF16), tri) + base_sc[...]
    r1 = jnp.sum(jnp.where(oh1, before, 0.0), axis=0, keepdims=True)
    r2 = jnp.sum(jnp.where(oh2, before, 0.0), axis=0, keepdims=True)
    base_sc[...] = base_sc[...] + jnp.sum(oh, axis=1, keepdims=True)

    eidx_ref[0:1, :] = i1
    eidx_ref[1:2, :] = i2
    rank_ref[0:1, :] = r1.astype(I32)
    rank_ref[1:2, :] = r2.astype(I32)
    wts_ref[0:1, :] = w1
    wts_ref[1:2, :] = w2
    cnt_ref[...] = jnp.broadcast_to(base_sc[...], cnt_ref.shape).astype(I32)


def _router(h, wr_t, cfg):
    T, D = h.shape
    E = wr_t.shape[0]
    TMR = cfg["TM"]
    col = lambda i: (0, i)
    return pl.pallas_call(
        functools.partial(_router_kernel, E=E, TMR=TMR),
        grid=(T // TMR,),
        in_specs=[pl.BlockSpec((TMR, D), lambda i: (i, 0)),
                  pl.BlockSpec((E, D), lambda i: (0, 0))],
        out_specs=[pl.BlockSpec((TOP_K, TMR), col), pl.BlockSpec((TOP_K, TMR), col),
                   pl.BlockSpec((TOP_K, TMR), col), pl.BlockSpec((E, LANES), lambda i: (0, 0))],
        out_shape=[jax.ShapeDtypeStruct((TOP_K, T), I32), jax.ShapeDtypeStruct((TOP_K, T), I32),
                   jax.ShapeDtypeStruct((TOP_K, T), F32), jax.ShapeDtypeStruct((E, LANES), I32)],
        scratch_shapes=[pltpu.VMEM((E, 1), F32)],
        compiler_params=_cp(("arbitrary",)),
        name="moe_router",
    )(h, wr_t)


ROW_DMA_UNROLL = 8


def _row_dma_start(row_copy, n_rows):
    def start(rb, c):
        for u in range(ROW_DMA_UNROLL):
            for k in range(TOP_K):
                row_copy(rb * ROW_DMA_UNROLL + u, k).start(priority=(u * TOP_K + k) % 2)
        return c

    lax.fori_loop(0, n_rows // ROW_DMA_UNROLL, start, 0)


def _row_dma_wait(row_copy, n_rows):
    def wait(rb, c):
        for u in range(ROW_DMA_UNROLL):
            for k in range(TOP_K):
                row_copy(rb * ROW_DMA_UNROLL + u, k).wait()
        return c

    lax.fori_loop(0, n_rows // ROW_DMA_UNROLL, wait, 0)


def _dispatch_kernel(dest_ref, h_ref, xs_in_ref, xs_ref, sem, *, T, TMD):
    del xs_in_ref
    base = pl.program_id(0) * TMD

    def row_copy(r, k):
        return pltpu.make_async_copy(h_ref.at[pl.ds(r, 1)],
                                     xs_ref.at[pl.ds(dest_ref[k * T + base + r], 1)], sem)

    _row_dma_start(row_copy, TMD)
    _row_dma_wait(row_copy, TMD)


def _dispatch(dest_flat, h, P, cfg):
    T, D = h.shape
    TMD = cfg["TM"]
    xs0 = jnp.zeros((P, D), h.dtype)
    return pl.pallas_call(
        functools.partial(_dispatch_kernel, T=T, TMD=TMD),
        grid_spec=pltpu.PrefetchScalarGridSpec(
            num_scalar_prefetch=1,
            grid=(T // TMD,),
            in_specs=[pl.BlockSpec((TMD, D), lambda i, dest: (i, 0)),
                      pl.BlockSpec(memory_space=pl.ANY)],
            out_specs=pl.BlockSpec(memory_space=pl.ANY),
            scratch_shapes=[pltpu.SemaphoreType.DMA(())]),
        out_shape=jax.ShapeDtypeStruct((P, D), h.dtype),
        input_output_aliases={2: 0},
        compiler_params=pltpu.CompilerParams(dimension_semantics=("arbitrary",),
                                             has_side_effects=True),
        name="moe_dispatch",
    )(dest_flat, h, xs0)


def _moe_kernel(te_ref, nu_ref, xs_ref, wg_ref, wu_ref, wd_ref, y_ref, acc_sc, xb_sc):
    j = pl.program_id(0)
    f = pl.program_id(1)
    used = j < nu_ref[0]
    last = f == pl.num_programs(1) - 1

    @pl.when(jnp.logical_and(used, f == 0))
    def _():
        acc_sc[...] = jnp.zeros(acc_sc.shape, F32)
        xb_sc[...] = xs_ref[...].astype(BF16)

    @pl.when(used)
    def _():
        acc_sc[...] += _dot(_swiglu_mid(xb_sc[...], wg_ref, wu_ref), wd_ref[...])

    @pl.when(jnp.logical_and(used, last))
    def _():
        y_ref[...] = acc_sc[...]

    @pl.when(jnp.logical_and(jnp.logical_not(used), last))
    def _():
        y_ref[...] = jnp.zeros(y_ref.shape, F32)


def _moe_ffn(tile_expert, n_used, xs, wg, wu, wd, layer, cfg):
    P, D = xs.shape
    DFE = wd.shape[2]
    TME, tf = cfg["TME"], cfg["TFE"]
    nf = DFE // tf

    def jj(j, nu):
        return jnp.minimum(j, nu[0] - 1)

    def ff(j, f, nu):
        return jnp.where(j < nu[0], f, nf - 1)

    return pl.pallas_call(
        _moe_kernel,
        grid_spec=pltpu.PrefetchScalarGridSpec(
            num_scalar_prefetch=2,
            grid=(P // TME, nf),
            in_specs=[pl.BlockSpec((TME, D), lambda j, f, te, nu: (jj(j, nu), 0)),
                      pl.BlockSpec((None, None, D, tf),
                                   lambda j, f, te, nu: (layer, te[jj(j, nu)], 0, ff(j, f, nu))),
                      pl.BlockSpec((None, None, D, tf),
                                   lambda j, f, te, nu: (layer, te[jj(j, nu)], 0, ff(j, f, nu))),
                      pl.BlockSpec((None, None, tf, D),
                                   lambda j, f, te, nu: (layer, te[jj(j, nu)], ff(j, f, nu), 0))],
            out_specs=pl.BlockSpec((TME, D), lambda j, f, te, nu: (j, 0)),
            scratch_shapes=[pltpu.VMEM((TME, D), F32), pltpu.VMEM((TME, D), BF16)]),
        out_shape=jax.ShapeDtypeStruct((P, D), F32),
        compiler_params=_cp(("arbitrary", "arbitrary")),
        name="moe_expert_ffn",
    )(tile_expert, n_used, xs, wg, wu, wd)


def _combine_kernel(dest_ref, y_ref, wts_ref, x1_ref, mod_ref, gpost_ref, gnext_ref, modn_ref,
                    x2_ref, hn_ref, ybuf, sem, *, T, TMC, D):
    i = pl.program_id(0)
    slot = i % 2

    def gather(tile, dst_slot):
        def row_copy(r, k):
            return pltpu.make_async_copy(y_ref.at[pl.ds(dest_ref[k * T + tile * TMC + r], 1)],
                                         ybuf.at[dst_slot, k, pl.ds(r, 1)], sem.at[dst_slot])
        return row_copy

    @pl.when(i == 0)
    def _():
        _row_dma_start(gather(0, 0), TMC)

    @pl.when(i + 1 < pl.num_programs(0))
    def _():
        _row_dma_start(gather(i + 1, 1 - slot), TMC)

    _row_dma_wait(gather(i, slot), TMC)
    w = wts_ref[...]
    y = w[:, 0:1] * ybuf[slot, 0] + w[:, 1:2] * ybuf[slot, 1]
    _ffn_epilogue(y, x1_ref, mod_ref, gpost_ref, gnext_ref, modn_ref, x2_ref, hn_ref, D)


def _combine(dest_flat, y, wts, x1, mod, gpost, gnext, modn, cfg):
    T, D = x1.shape
    TMC = cfg["TM_MERGE"]
    row = lambda i, d: (i, 0)
    const = lambda i, d: (0, 0)
    modrow = lambda i, d: (cfg["mod_row"](i, TMC), 0, 0)
    return pl.pallas_call(
        functools.partial(_combine_kernel, T=T, TMC=TMC, D=D),
        grid_spec=pltpu.PrefetchScalarGridSpec(
            num_scalar_prefetch=1,
            grid=(T // TMC,),
            in_specs=[pl.BlockSpec(memory_space=pl.ANY),
                      pl.BlockSpec((TMC, TOP_K), row),
                      pl.BlockSpec((TMC, D), row),
                      pl.BlockSpec((1, 1, 6 * D), modrow),
                      pl.BlockSpec((1, D), const), pl.BlockSpec((1, D), const),
                      pl.BlockSpec((1, 1, 6 * D), modrow)],
            out_specs=[pl.BlockSpec((TMC, D), row), pl.BlockSpec((TMC, D), row)],
            scratch_shapes=[pltpu.VMEM((2, TOP_K, TMC, D), F32), pltpu.SemaphoreType.DMA((2,))]),
        out_shape=[jax.ShapeDtypeStruct((T, D), F32), jax.ShapeDtypeStruct((T, D), BF16)],
        compiler_params=_cp(("arbitrary",)),
        name="moe_combine",
    )(dest_flat, y, wts, x1, mod, gpost, gnext, modn)


def _rope_tables(Ss, TM):
    n_freq = LANES // 4
    t = jnp.arange(Ss, dtype=I32)
    rows = (t // GRID_W).astype(F32)
    cols = (t % GRID_W).astype(F32)
    inv_freq = jnp.power(ROPE_THETA, -jnp.arange(n_freq, dtype=F32) / n_freq)
    ar = rows[:, None] * inv_freq
    ac = cols[:, None] * inv_freq
    cos = jnp.concatenate([jnp.cos(ar), jnp.cos(ar), jnp.cos(ac), jnp.cos(ac)], axis=-1)
    sin = jnp.concatenate([-jnp.sin(ar), jnp.sin(ar), -jnp.sin(ac), jnp.sin(ac)], axis=-1)
    cos = jnp.concatenate([jnp.ones((TM, LANES), F32), cos], axis=0)
    sin = jnp.concatenate([jnp.zeros((TM, LANES), F32), sin], axis=0)
    return cos, sin


def _dft_tables(n):
    j = jnp.arange(n, dtype=I32)
    ph = (j[:, None] * j[None, :]) % n
    ang = ph.astype(F32) * (2.0 * math.pi / n)
    return jnp.cos(ang).astype(BF16), jnp.sin(ang).astype(BF16)


def kernel(x_prompt, x_sample, cache_k, cache_v, c, c_ctx, w_mod, b_mod, g_pre_mix, g_post_mix,
           g_pre_ffn, g_post_ffn, w_in, g_q, g_k, w_pool_map, pool_scale, w_attn_o, w_pool_o,
           w_four_o, w_out, w_ffn_gate, w_ffn_up, w_ffn_down, w_router, w_exp_gate, w_exp_up,
           w_exp_down):
    Bp, Sp, D = x_prompt.shape
    Bs, Ss, _ = x_sample.shape
    L = w_mod.shape[0]
    PAST, KV = cache_k.shape[2], cache_k.shape[3]
    AW, PW, FW = w_attn_o.shape[1], w_pool_o.shape[1], w_four_o.shape[1]
    NH = AW // LANES
    G = NH // KV
    KW = KV * LANES
    E = w_router.shape[2]
    NP, NS = Bp * Sp, Bs * Ss
    T = NP + NS
    TM = min(512, math.gcd(NP, Ss))
    assert cache_k.shape[4] == LANES and NP % TM == 0 and Ss % TM == 0 and NP % Ss == 0
    assert Ss % Sp == 0 and Sp & (Sp - 1) == 0 and Ss & (Ss - 1) == 0 and Ss % GRID_W == 0
    NPT, TPS = NP // TM, Ss // TM
    TM_BIG = min(1024, math.gcd(NP, Ss))
    TME = TM
    P = (-(-(T * TOP_K) // TME) + E) * TME

    pick_tile = lambda n, cands: next(t for t in cands if n % t == 0)
    cfg = dict(
        TM=TM, TM_MERGE=min(TM, 256), TME=TME, TM_FFN=TM, TM_GATES=TM_BIG,
        TF=pick_tile(w_ffn_gate.shape[2], (512, 256, 128)),
        TFE=pick_tile(w_exp_gate.shape[3], (1024, 512, 256, 128)),
        NH=NH, KV=KV, PW=PW, FW=FW, NP=NP, Sp=Sp, Ss=Ss,
        mod_row=lambda i, tm=TM: jnp.where(i * tm < NP, 0, 1 + (i * tm - NP) // Ss),
        rope_blk=lambda i: jnp.where(i < NPT, 0, 1 + (i - NPT) % TPS),
    )

    x = jnp.concatenate([x_prompt.reshape(NP, D), x_sample.reshape(NS, D)], axis=0)
    R = -(-(1 + Bs) // 8) * 8
    cond = jnp.zeros((R, D), F32).at[0].set(c_ctx).at[1:1 + Bs].set(c)
    mod_all = _modulation(cond, w_mod, b_mod).reshape(L, R, 1, 6 * D)

    o1, o2, o3, o4 = AW, AW + 2 * KW, AW + 2 * KW + PW, AW + 2 * KW + PW + FW
    w_in_b = w_in.astype(BF16)
    cos_tab, sin_tab = _rope_tables(Ss, TM)
    cc, sc = _dft_tables(LANES)
    cs_chan = jnp.concatenate([cc, sc], axis=1)
    dft_p = _dft_tables(Sp)
    dft_s = _dft_tables(Ss)
    ck = cache_k.astype(BF16).reshape(Bs, L, PAST, KW)
    cv = cache_v.astype(BF16).reshape(Bs, L, PAST, KW)
    row1 = lambda a: a.reshape(1, -1)
    dense_w = [w.astype(BF16) for w in (w_ffn_gate, w_ffn_up, w_ffn_down)]
    expert_w = [w.astype(BF16) for w in (w_exp_gate, w_exp_up, w_exp_down)]

    h = _norm_mod(x, row1(g_pre_mix[0]), mod_all[0], cfg)
    new_k, new_v = [], []
    for l in range(L):
        mod = mod_all[l]
        q, kb, vb, kf, vf = _qkv(h, w_in_b[l, :, :o2], row1(g_q[l]), row1(g_k[l]), cos_tab, sin_tab, cfg)
        u_pool, fa, fb = _pool_fourier_proj(h, w_in_b[l, :, o2:o4], cs_chan, cfg)
        gates = _gates(h, w_in_b[l, :, o4:], cfg)
        new_k.append(kf[:NP].reshape(Bp, Sp, KV, LANES))
        new_v.append(vf[:NP].reshape(Bp, Sp, KV, LANES))

        attn_p = _attention(q, kb, vb, q_row0=0, B=Bp, S=Sp, Tk=Sp, KV=KV, G=G)
        k_s = jnp.concatenate([ck[:, l], kb[NP:].reshape(Bs, Ss, KW)], axis=1).reshape(-1, KW)
        v_s = jnp.concatenate([cv[:, l], vb[NP:].reshape(Bs, Ss, KW)], axis=1).reshape(-1, KW)
        attn_s = _attention(q, k_s, v_s, q_row0=NP, B=Bs, S=Ss, Tk=PAST + Ss, KV=KV, G=G)

        pool = _pool(u_pool, w_pool_map[l].astype(BF16), row1(pool_scale[l]), cfg)
        four_p = _fourier(fa, fb, *dft_p, row0=0, B=Bp, S=Sp)
        four_s = _fourier(fa, fb, *dft_s, row0=NP, B=Bs, S=Ss)

        x1, h2 = _merge(attn_p, attn_s, pool, four_p, four_s, gates, x, mod, w_attn_o[l].astype(BF16),
                        w_pool_o[l].astype(BF16), w_four_o[l].astype(BF16), w_out[l].astype(BF16),
                        row1(g_post_mix[l]), row1(g_pre_ffn[l]), BF16 if l % 2 == 0 else F32, cfg)

        ln = min(l + 1, L - 1)
        gnext, modn = row1(g_pre_mix[ln]), mod_all[ln]
        i = l // 2
        if l % 2 == 0:
            x, h = _dense_ffn(h2, *dense_w, i, x1, mod, row1(g_post_ffn[l]), gnext, modn, cfg)
        else:
            eidx, rank, wts, cnt = _router(h2, w_router[i].T, cfg)
            counts = cnt[:, 0]
            padded = (counts + TME - 1) // TME * TME
            pad_end = jnp.cumsum(padded)
            pad_start = pad_end - padded
            start_of = sum(jnp.where(eidx == e, pad_start[e], 0) for e in range(E))
            dest = (start_of + rank).reshape(-1).astype(I32)
            n_used = (pad_end[-1] // TME).astype(I32).reshape(1)
            tile_start = jnp.arange(P // TME, dtype=I32) * TME
            tile_expert = jnp.minimum(
                jnp.sum(tile_start[:, None] >= pad_end[None, :], axis=1), E - 1).astype(I32)
            xs = _dispatch(dest, h2, P, cfg)
            y = _moe_ffn(tile_expert, n_used, xs, *expert_w, i, cfg)
            x, h = _combine(dest, y, wts.T, x1, mod, row1(g_post_ffn[l]), gnext, modn, cfg)

    y_prompt = x[:NP].reshape(Bp, Sp, D)
    y_sample = x[NP:].reshape(Bs, Ss, D)
    return (y_prompt, y_sample, jnp.stack(new_k, axis=1), jnp.stack(new_v, axis=1))
```

```python
import functools
import math

import jax
import jax.numpy as jnp
from jax import lax
from jax.experimental import pallas as pl
from jax.experimental.pallas import tpu as pltpu

F32 = jnp.float32
BF16 = jnp.bfloat16
I32 = jnp.int32

LANES = 128
GRID_W = 64
ROPE_THETA = 10000.0
EPS = 1e-6
TOP_K = 2
POOL_WINDOWS = (2, 4, 8, 16)
VMEM_LIMIT = 56 * 1024 * 1024


def _cp(sem, vmem=VMEM_LIMIT):
    return pltpu.CompilerParams(dimension_semantics=sem, vmem_limit_bytes=vmem)


def _rms(x, g):
    return x * lax.rsqrt(jnp.mean(x * x, axis=-1, keepdims=True) + EPS) * g


def _silu(x):
    return x / (1.0 + jnp.exp(-x))


def _dot(a, b):
    return jnp.dot(a, b, preferred_element_type=F32)


def _resident(shape, index_map):
    return pl.BlockSpec(shape, index_map, pipeline_mode=pl.Buffered(1))


def _mod_kernel(c_ref, w_ref, b_ref, o_ref):
    s = _silu(c_ref[...])
    o_ref[0] = jnp.dot(s, w_ref[0], preferred_element_type=F32,
                       precision=lax.Precision.HIGHEST) + b_ref[0]


def _modulation(cond, w_mod, b_mod):
    L, D, N = w_mod.shape
    R = cond.shape[0]
    tn = 1024
    return pl.pallas_call(
        _mod_kernel,
        grid=(L, N // tn),
        in_specs=[pl.BlockSpec((R, D), lambda l, j: (0, 0)),
                  pl.BlockSpec((1, D, tn), lambda l, j: (l, 0, j)),
                  pl.BlockSpec((1, 1, tn), lambda l, j: (l, 0, j))],
        out_specs=pl.BlockSpec((1, R, tn), lambda l, j: (l, 0, j)),
        out_shape=jax.ShapeDtypeStruct((L, R, N), F32),
        compiler_params=_cp(("arbitrary", "arbitrary")),
        name="modulation",
    )(cond, w_mod, b_mod.reshape(L, 1, N))


def _norm_mod_kernel(x_ref, g_ref, mod_ref, h_ref, *, D):
    mod = mod_ref[0]
    y = _rms(x_ref[...], g_ref[...])
    h_ref[...] = (y * (1.0 + mod[:, D:2 * D]) + mod[:, 0:D]).astype(BF16)


def _norm_mod(x, g, mod, cfg):
    T, D = x.shape
    TM = cfg["TM"]
    return pl.pallas_call(
        functools.partial(_norm_mod_kernel, D=D),
        grid=(T // TM,),
        in_specs=[pl.BlockSpec((TM, D), lambda i: (i, 0)),
                  pl.BlockSpec((1, D), lambda i: (0, 0)),
                  pl.BlockSpec((1, 1, 6 * D), lambda i: (cfg["mod_row"](i), 0, 0))],
        out_specs=pl.BlockSpec((TM, D), lambda i: (i, 0)),
        out_shape=jax.ShapeDtypeStruct((T, D), BF16),
        compiler_params=_cp(("arbitrary",)),
        name="norm_mod",
    )(x, g, mod)


def _qkv_kernel(h_ref, w_ref, gq_ref, gk_ref, cos_ref, sin_ref,
                q_ref, kb_ref, vb_ref, kf_ref, vf_ref, *, NH, KV, scale):
    acc = _dot(h_ref[...], w_ref[...])
    cos = cos_ref[...]
    sin = sin_ref[...]
    lane = lax.broadcasted_iota(I32, cos.shape, 1)
    first_half = (lane % (LANES // 2)) < (LANES // 4)

    def rope(y):
        partner = jnp.where(first_half, pltpu.roll(y, LANES - LANES // 4, axis=1),
                            pltpu.roll(y, LANES // 4, axis=1))
        return y * cos + partner * sin

    for hd in range(NH):
        z = acc[:, hd * LANES:(hd + 1) * LANES]
        q_ref[:, hd * LANES:(hd + 1) * LANES] = (rope(_rms(z, gq_ref[...])) * scale).astype(BF16)
    for hd in range(KV):
        z = acc[:, (NH + hd) * LANES:(NH + hd + 1) * LANES]
        y = _rms(z, gk_ref[...])
        kf_ref[:, hd * LANES:(hd + 1) * LANES] = y
        kb_ref[:, hd * LANES:(hd + 1) * LANES] = rope(y).astype(BF16)
    v = acc[:, (NH + KV) * LANES:(NH + 2 * KV) * LANES]
    vf_ref[...] = v
    vb_ref[...] = v.astype(BF16)


def _qkv(h, w, gq, gk, cos_tab, sin_tab, cfg):
    T, D = h.shape
    TM, NH, KV = cfg["TM"], cfg["NH"], cfg["KV"]
    AW, KW = NH * LANES, KV * LANES
    row = lambda i: (i, 0)
    return pl.pallas_call(
        functools.partial(_qkv_kernel, NH=NH, KV=KV, scale=float(LANES) ** -0.5 * math.log2(math.e)),
        grid=(T // TM,),
        in_specs=[pl.BlockSpec((TM, D), row),
                  _resident((D, AW + 2 * KW), lambda i: (0, 0)),
                  pl.BlockSpec((1, LANES), lambda i: (0, 0)),
                  pl.BlockSpec((1, LANES), lambda i: (0, 0)),
                  pl.BlockSpec((TM, LANES), lambda i: (cfg["rope_blk"](i), 0)),
                  pl.BlockSpec((TM, LANES), lambda i: (cfg["rope_blk"](i), 0))],
        out_specs=[pl.BlockSpec((TM, AW), row), pl.BlockSpec((TM, KW), row),
                   pl.BlockSpec((TM, KW), row), pl.BlockSpec((TM, KW), row),
                   pl.BlockSpec((TM, KW), row)],
        out_shape=[jax.ShapeDtypeStruct((T, AW), BF16), jax.ShapeDtypeStruct((T, KW), BF16),
                   jax.ShapeDtypeStruct((T, KW), BF16), jax.ShapeDtypeStruct((T, KW), F32),
                   jax.ShapeDtypeStruct((T, KW), F32)],
        compiler_params=_cp(("arbitrary",)),
        name="qkv_proj",
    )(h, w, gq, gk, cos_tab, sin_tab)


def _pf_kernel(h_ref, w_ref, cs_ref, up_ref, a_ref, b_ref, *, PW, NG):
    acc = _dot(h_ref[...], w_ref[...])
    up_ref[...] = acc[:, :PW]
    for g in range(NG):
        u = acc[:, PW + g * LANES:PW + (g + 1) * LANES].astype(BF16)
        ab = _dot(u, cs_ref[...])
        a_ref[:, g * LANES:(g + 1) * LANES] = ab[:, :LANES].astype(BF16)
        b_ref[:, g * LANES:(g + 1) * LANES] = ab[:, LANES:].astype(BF16)


def _pool_fourier_proj(h, w, cs, cfg):
    T, D = h.shape
    TM, PW, FW = cfg["TM"], cfg["PW"], cfg["FW"]
    row = lambda i: (i, 0)
    return pl.pallas_call(
        functools.partial(_pf_kernel, PW=PW, NG=FW // LANES),
        grid=(T // TM,),
        in_specs=[pl.BlockSpec((TM, D), row),
                  _resident((D, PW + FW), lambda i: (0, 0)),
                  pl.BlockSpec((LANES, 2 * LANES), lambda i: (0, 0))],
        out_specs=[pl.BlockSpec((TM, PW), row), pl.BlockSpec((TM, FW), row),
                   pl.BlockSpec((TM, FW), row)],
        out_shape=[jax.ShapeDtypeStruct((T, PW), F32), jax.ShapeDtypeStruct((T, FW), BF16),
                   jax.ShapeDtypeStruct((T, FW), BF16)],
        compiler_params=_cp(("arbitrary",)),
        name="pool_fourier_proj",
    )(h, w, cs)


def _gates_kernel(h_ref, w_ref, o_ref):
    z = _dot(h_ref[...], w_ref[...])
    o_ref[...] = (1.0 / (1.0 + jnp.exp(-z))).astype(BF16)


def _gates(h, w, cfg):
    T, D = h.shape
    N = w.shape[1]
    TM = cfg["TM_GATES"]
    tn = 1024
    return pl.pallas_call(
        _gates_kernel,
        grid=(N // tn, T // TM),
        in_specs=[pl.BlockSpec((TM, D), lambda j, i: (i, 0)),
                  pl.BlockSpec((D, tn), lambda j, i: (0, j))],
        out_specs=pl.BlockSpec((TM, tn), lambda j, i: (i, j)),
        out_shape=jax.ShapeDtypeStruct((T, N), BF16),
        compiler_params=_cp(("arbitrary", "arbitrary")),
        name="gates_proj",
    )(h, w)


def _attn_kernel(q_ref, k_ref, v_ref, o_ref, *, G, tq, tk, nk):
    q = jnp.concatenate([q_ref[:, g * LANES:(g + 1) * LANES] for g in range(G)], axis=0)
    rows = G * tq
    nt = tk // LANES
    m = jnp.full((rows, LANES), -jnp.inf, F32)
    l = jnp.zeros((rows, LANES), F32)
    acc = jnp.zeros((rows, LANES), F32)
    for c in range(nk):
        k = k_ref[c * tk:(c + 1) * tk, :]
        v = v_ref[c * tk:(c + 1) * tk, :]
        s = lax.dot_general(q, k, (((1,), (1,)), ((), ())), preferred_element_type=F32)
        m_new = jnp.maximum(m, jnp.max(s, axis=-1, keepdims=True))
        alpha = jnp.exp2(m - m_new)
        p = [jnp.exp2(s[:, t * LANES:(t + 1) * LANES] - m_new) for t in range(nt)]
        l = alpha * l + functools.reduce(lambda a, b: a + b, p)
        pb = jnp.concatenate([pt.astype(BF16) for pt in p], axis=1)
        acc = alpha * acc + _dot(pb, v)
        m = m_new
    o = acc / jnp.sum(l, axis=-1, keepdims=True)
    for g in range(G):
        o_ref[:, g * LANES:(g + 1) * LANES] = o[g * tq:(g + 1) * tq].astype(BF16)


def _attention(q, k, v, *, q_row0, B, S, Tk, KV, G):
    tq = min(256, S)
    tk = next(t for t in (512, 256, 128) if Tk % t == 0)
    nq, nk = S // tq, Tk // tk
    qb0 = q_row0 // tq
    return pl.pallas_call(
        functools.partial(_attn_kernel, G=G, tq=tq, tk=tk, nk=nk),
        grid=(B, KV, nq),
        in_specs=[pl.BlockSpec((tq, G * LANES), lambda b, h, qi: (qb0 + b * nq + qi, h)),
                  pl.BlockSpec((Tk, LANES), lambda b, h, qi: (b, h)),
                  pl.BlockSpec((Tk, LANES), lambda b, h, qi: (b, h))],
        out_specs=pl.BlockSpec((tq, G * LANES), lambda b, h, qi: (b * nq + qi, h)),
        out_shape=jax.ShapeDtypeStruct((B * S, KV * G * LANES), BF16),
        compiler_params=_cp(("arbitrary",) * 3),
        name="attention",
    )(q, k, v)


def _pool_kernel(u_ref, wmap_ref, scale_ref, o_ref, tot_sc, *, R, n_prompt_blocks, Sp, Ss):
    blk = pl.program_id(0)
    g = pl.program_id(1)
    S = jnp.where(blk < n_prompt_blocks, Sp, Ss)
    half = jnp.left_shift(1, g)
    pos = lax.broadcasted_iota(I32, (R, LANES), 0) & (S - 1)

    def add_offsets(offsets):
        u = u_ref[...]
        tot = tot_sc[...]
        for k in offsets:
            valid = jnp.logical_and(pos + k >= 0, pos + k < S)
            tot = tot + jnp.where(valid, pltpu.roll(u, (-k) % R, axis=0), 0.0)
        tot_sc[...] = tot

    tot_sc[...] = u_ref[...]
    add_offsets([-1])
    for lvl in range(1, len(POOL_WINDOWS)):
        h0, h1 = POOL_WINDOWS[lvl - 1] // 2, POOL_WINDOWS[lvl] // 2

        @pl.when(g >= lvl)
        def _():
            add_offsets(list(range(-h1, -h0)) + list(range(h0, h1)))

    cnt = (jnp.minimum(pos + half, S) - jnp.maximum(pos - half, 0)).astype(F32)
    pooled = tot_sc[...] / cnt - u_ref[...]
    mixed = _dot(pooled.astype(BF16), wmap_ref[0]) * scale_ref[...]
    o_ref[...] = mixed.astype(BF16)


def _pool(u, wmap, scale, cfg):
    T, PW = u.shape
    R, NP, Sp, Ss = cfg["Ss"], cfg["NP"], cfg["Sp"], cfg["Ss"]
    return pl.pallas_call(
        functools.partial(_pool_kernel, R=R, n_prompt_blocks=NP // R, Sp=Sp, Ss=Ss),
        grid=(T // R, PW // LANES),
        in_specs=[pl.BlockSpec((R, LANES), lambda b, g: (b, g)),
                  pl.BlockSpec((1, LANES, LANES), lambda b, g: (g, 0, 0)),
                  pl.BlockSpec((1, LANES), lambda b, g: (0, g))],
        out_specs=pl.BlockSpec((R, LANES), lambda b, g: (b, g)),
        out_shape=jax.ShapeDtypeStruct((T, PW), BF16),
        scratch_shapes=[pltpu.VMEM((R, LANES), F32)],
        compiler_params=_cp(("arbitrary", "arbitrary")),
        name="multiscale_pool",
    )(u, wmap, scale)


def _fourier_kernel(cs_ref, ss_ref, a_ref, b_ref, o_ref, acc_sc, *, scale):
    k = pl.program_id(2)

    @pl.when(k == 0)
    def _():
        acc_sc[...] = jnp.zeros(acc_sc.shape, F32)

    acc_sc[...] += _dot(cs_ref[...], a_ref[...]) - _dot(ss_ref[...], b_ref[...])

    @pl.when(k == pl.num_programs(2) - 1)
    def _():
        o_ref[...] = (acc_sc[...] * scale).astype(BF16)


def _fourier(a, b, cs, ss, *, row0, B, S):
    FW = a.shape[1]
    tm = min(1024, S)
    tk = min(1024, S)
    nm, nk = S // tm, S // tk
    kb0 = row0 // tk
    return pl.pallas_call(
        functools.partial(_fourier_kernel, scale=float(S * LANES) ** -0.5),
        grid=(B, nm, nk),
        in_specs=[pl.BlockSpec((tm, tk), lambda bb, i, k: (i, k)),
                  pl.BlockSpec((tm, tk), lambda bb, i, k: (i, k)),
                  pl.BlockSpec((tk, FW), lambda bb, i, k: (kb0 + bb * nk + k, 0)),
                  pl.BlockSpec((tk, FW), lambda bb, i, k: (kb0 + bb * nk + k, 0))],
        out_specs=pl.BlockSpec((tm, FW), lambda bb, i, k: (bb * nm + i, 0)),
        out_shape=jax.ShapeDtypeStruct((B * S, FW), BF16),
        scratch_shapes=[pltpu.VMEM((tm, FW), F32)],
        compiler_params=_cp(("arbitrary",) * 3),
        name="fourier_mix",
    )(cs, ss, a, b)


def _merge_kernel(attn_p_ref, attn_s_ref, pool_ref, four_p_ref, four_s_ref, gates_ref, x_ref, mod_ref,
                  wa_ref, wp_ref, wf_ref, wo_ref, gpost_ref, gpre_ref, x1_ref, h2_ref, *, D, n_prompt_tiles):
    mod = mod_ref[0]
    is_prompt = pl.program_id(0) < n_prompt_tiles
    attn = jnp.where(is_prompt, attn_p_ref[...], attn_s_ref[...])
    four = jnp.where(is_prompt, four_p_ref[...], four_s_ref[...])
    merged = (gates_ref[:, 0:D].astype(F32) * _dot(attn, wa_ref[...])
              + gates_ref[:, D:2 * D].astype(F32) * _dot(pool_ref[...], wp_ref[...])
              + gates_ref[:, 2 * D:3 * D].astype(F32) * _dot(four, wf_ref[...]))
    y = _dot(merged.astype(BF16), wo_ref[...])
    x1 = x_ref[...] + mod[:, 2 * D:3 * D] * _rms(y, gpost_ref[...])
    x1_ref[...] = x1
    h2 = _rms(x1, gpre_ref[...]) * (1.0 + mod[:, 4 * D:5 * D]) + mod[:, 3 * D:4 * D]
    h2_ref[...] = h2.astype(h2_ref.dtype)


def _merge(attn_p, attn_s, pool, four_p, four_s, gates, x, mod, wa, wp, wf, wo, gpost, gpre, h2_dtype, cfg):
    T, D = x.shape
    TM = cfg["TM_MERGE"]
    npt = cfg["NP"] // TM
    row = lambda i: (i, 0)
    const = lambda i: (0, 0)
    prow = lambda i: (jnp.minimum(i, npt - 1), 0)
    srow = lambda i: (jnp.maximum(i - npt, 0), 0)
    AW, FW = attn_p.shape[1], four_p.shape[1]
    return pl.pallas_call(
        functools.partial(_merge_kernel, D=D, n_prompt_tiles=npt),
        grid=(T // TM,),
        in_specs=[pl.BlockSpec((TM, AW), prow), pl.BlockSpec((TM, AW), srow),
                  pl.BlockSpec((TM, pool.shape[1]), row),
                  pl.BlockSpec((TM, FW), prow), pl.BlockSpec((TM, FW), srow),
                  pl.BlockSpec((TM, 3 * D), row),
                  pl.BlockSpec((TM, D), row),
                  pl.BlockSpec((1, 1, 6 * D), lambda i: (cfg["mod_row"](i, TM), 0, 0)),
                  _resident(wa.shape, const), _resident(wp.shape, const),
                  _resident(wf.shape, const), _resident(wo.shape, const),
                  pl.BlockSpec((1, D), const), pl.BlockSpec((1, D), const)],
        out_specs=[pl.BlockSpec((TM, D), row), pl.BlockSpec((TM, D), row)],
        out_shape=[jax.ShapeDtypeStruct((T, D), F32), jax.ShapeDtypeStruct((T, D), h2_dtype)],
        compiler_params=_cp(("arbitrary",)),
        name="merge_out_proj",
    )(attn_p, attn_s, pool, four_p, four_s, gates, x, mod, wa, wp, wf, wo, gpost, gpre)


def _ffn_epilogue(y, x1_ref, mod_ref, gpost_ref, gnext_ref, modn_ref, x2_ref, hn_ref, D):
    mod = mod_ref[0]
    x2 = x1_ref[...] + mod[:, 5 * D:6 * D] * _rms(y, gpost_ref[...])
    x2_ref[...] = x2
    modn = modn_ref[0]
    hn = _rms(x2, gnext_ref[...]) * (1.0 + modn[:, D:2 * D]) + modn[:, 0:D]
    hn_ref[...] = hn.astype(BF16)


def _swiglu_mid(x, wg_ref, wu_ref):
    return (_silu(_dot(x, wg_ref[...])) * _dot(x, wu_ref[...])).astype(BF16)


def _ffn_kernel(h_ref, wg_ref, wu_ref, wd_ref, x1_ref, mod_ref, gpost_ref, gnext_ref, modn_ref,
                x2_ref, hn_ref, acc_sc, *, D):
    f = pl.program_id(1)

    @pl.when(f == 0)
    def _():
        acc_sc[...] = jnp.zeros(acc_sc.shape, F32)

    acc_sc[...] += _dot(_swiglu_mid(h_ref[...], wg_ref, wu_ref), wd_ref[...])

    @pl.when(f == pl.num_programs(1) - 1)
    def _():
        _ffn_epilogue(acc_sc[...], x1_ref, mod_ref, gpost_ref, gnext_ref, modn_ref, x2_ref, hn_ref, D)


def _dense_ffn(h, wg, wu, wd, layer, x1, mod, gpost, gnext, modn, cfg):
    T, D = h.shape
    DFF = wd.shape[1]
    TM = cfg["TM_FFN"]
    tf = cfg["TF"]
    row = lambda i, f: (i, 0)
    const = lambda i, f: (0, 0)
    modrow = lambda i, f: (cfg["mod_row"](i, TM), 0, 0)
    return pl.pallas_call(
        functools.partial(_ffn_kernel, D=D),
        grid=(T // TM, DFF // tf),
        in_specs=[pl.BlockSpec((TM, D), row),
                  pl.BlockSpec((None, D, tf), lambda i, f: (layer, 0, f)),
                  pl.BlockSpec((None, D, tf), lambda i, f: (layer, 0, f)),
                  pl.BlockSpec((None, tf, D), lambda i, f: (layer, f, 0)),
                  pl.BlockSpec((TM, D), row),
                  pl.BlockSpec((1, 1, 6 * D), modrow),
                  pl.BlockSpec((1, D), const), pl.BlockSpec((1, D), const),
                  pl.BlockSpec((1, 1, 6 * D), modrow)],
        out_specs=[pl.BlockSpec((TM, D), row), pl.BlockSpec((TM, D), row)],
        out_shape=[jax.ShapeDtypeStruct((T, D), F32), jax.ShapeDtypeStruct((T, D), BF16)],
        scratch_shapes=[pltpu.VMEM((TM, D), F32)],
        compiler_params=_cp(("arbitrary", "arbitrary")),
        name="dense_ffn",
    )(h, wg, wu, wd, x1, mod, gpost, gnext, modn)


def _router_kernel(h_ref, wr_ref, eidx_ref, rank_ref, wts_ref, cnt_ref, base_sc, *, E, TMR):
    i = pl.program_id(0)

    @pl.when(i == 0)
    def _():
        base_sc[...] = jnp.zeros(base_sc.shape, F32)

    logits = lax.dot_general(wr_ref[...], h_ref[...].astype(F32), (((1,), (1,)), ((), ())),
                             preferred_element_type=F32, precision=lax.Precision.HIGHEST)
    e_iota = lax.broadcasted_iota(I32, (E, TMR), 0)
    m1 = jnp.max(logits, axis=0, keepdims=True)
    i1 = jnp.min(jnp.where(logits == m1, e_iota, E), axis=0, keepdims=True)
    rest = jnp.where(e_iota == i1, -jnp.inf, logits)
    m2 = jnp.max(rest, axis=0, keepdims=True)
    i2 = jnp.min(jnp.where(rest == m2, e_iota, E), axis=0, keepdims=True)
    ex = jnp.exp(m2 - m1)
    w1 = 1.0 / (1.0 + ex)
    w2 = ex / (1.0 + ex)

    oh1 = e_iota == i1
    oh2 = e_iota == i2
    oh = jnp.where(jnp.logical_or(oh1, oh2), 1.0, 0.0)
    tri = jnp.where(lax.broadcasted_iota(I32, (TMR, TMR), 0) < lax.broadcasted_iota(I32, (TMR, TMR), 1),
                    1.0, 0.0).astype(BF16)
    before = _dot(oh.astype(BF16), tri) + base_sc[...]
    r1 = jnp.sum(jnp.where(oh1, before, 0.0), axis=0, keepdims=True)
    r2 = jnp.sum(jnp.where(oh2, before, 0.0), axis=0, keepdims=True)
    base_sc[...] = base_sc[...] + jnp.sum(oh, axis=1, keepdims=True)

    eidx_ref[0:1, :] = i1
    eidx_ref[1:2, :] = i2
    rank_ref[0:1, :] = r1.astype(I32)
    rank_ref[1:2, :] = r2.astype(I32)
    wts_ref[0:1, :] = w1
    wts_ref[1:2, :] = w2
    cnt_ref[...] = jnp.broadcast_to(base_sc[...], cnt_ref.shape).astype(I32)


def _router(h, wr_t, cfg):
    T, D = h.shape
    E = wr_t.shape[0]
    TMR = cfg["TM"]
    col = lambda i: (0, i)
    return pl.pallas_call(
        functools.partial(_router_kernel, E=E, TMR=TMR),
        grid=(T // TMR,),
        in_specs=[pl.BlockSpec((TMR, D), lambda i: (i, 0)),
                  pl.BlockSpec((E, D), lambda i: (0, 0))],
        out_specs=[pl.BlockSpec((TOP_K, TMR), col), pl.BlockSpec((TOP_K, TMR), col),
                   pl.BlockSpec((TOP_K, TMR), col), pl.BlockSpec((E, LANES), lambda i: (0, 0))],
        out_shape=[jax.ShapeDtypeStruct((TOP_K, T), I32), jax.ShapeDtypeStruct((TOP_K, T), I32),
                   jax.ShapeDtypeStruct((TOP_K, T), F32), jax.ShapeDtypeStruct((E, LANES), I32)],
        scratch_shapes=[pltpu.VMEM((E, 1), F32)],
        compiler_params=_cp(("arbitrary",)),
        name="moe_router",
    )(h, wr_t)


ROW_DMA_UNROLL = 8


def _row_dma_start(row_copy, n_rows):
    def start(rb, c):
        for u in range(ROW_DMA_UNROLL):
            for k in range(TOP_K):
                row_copy(rb * ROW_DMA_UNROLL + u, k).start(priority=(u * TOP_K + k) % 2)
        return c

    lax.fori_loop(0, n_rows // ROW_DMA_UNROLL, start, 0)


def _row_dma_wait(row_copy, n_rows):
    def wait(rb, c):
        for u in range(ROW_DMA_UNROLL):
            for k in range(TOP_K):
                row_copy(rb * ROW_DMA_UNROLL + u, k).wait()
        return c

    lax.fori_loop(0, n_rows // ROW_DMA_UNROLL, wait, 0)


def _dispatch_kernel(dest_ref, pad_lo_ref, pad_hi_ref, h_ref, xs_ref, zero_row, sem, pad_sem,
                     *, T, TMD, n_pad_ranges):
    base = pl.program_id(0) * TMD

    @pl.when(pl.program_id(0) == 0)
    def _():
        zero_row[...] = jnp.zeros(zero_row.shape, zero_row.dtype)

        def pad_copy(r):
            return pltpu.make_async_copy(zero_row, xs_ref.at[pl.ds(r, 1)], pad_sem)

        def start(r, c):
            pad_copy(r).start()
            return c

        def wait(r, c):
            pad_copy(r).wait()
            return c

        for z in range(n_pad_ranges):
            lax.fori_loop(pad_lo_ref[z], pad_hi_ref[z], start, 0)
        for z in range(n_pad_ranges):
            lax.fori_loop(pad_lo_ref[z], pad_hi_ref[z], wait, 0)

    def row_copy(r, k):
        return pltpu.make_async_copy(h_ref.at[pl.ds(r, 1)],
                                     xs_ref.at[pl.ds(dest_ref[k * T + base + r], 1)], sem)

    _row_dma_start(row_copy, TMD)
    _row_dma_wait(row_copy, TMD)


def _dispatch(dest_flat, pad_lo, pad_hi, h, P, cfg):
    T, D = h.shape
    TMD = cfg["TM"]
    return pl.pallas_call(
        functools.partial(_dispatch_kernel, T=T, TMD=TMD, n_pad_ranges=pad_lo.shape[0]),
        grid_spec=pltpu.PrefetchScalarGridSpec(
            num_scalar_prefetch=3,
            grid=(T // TMD,),
            in_specs=[pl.BlockSpec((TMD, D), lambda i, dest, lo, hi: (i, 0))],
            out_specs=pl.BlockSpec(memory_space=pl.ANY),
            scratch_shapes=[pltpu.VMEM((1, D), h.dtype), pltpu.SemaphoreType.DMA(()),
                            pltpu.SemaphoreType.DMA(())]),
        out_shape=jax.ShapeDtypeStruct((P, D), h.dtype),
        compiler_params=pltpu.CompilerParams(dimension_semantics=("arbitrary",),
                                             has_side_effects=True),
        name="moe_dispatch",
    )(dest_flat, pad_lo, pad_hi, h)


def _moe_kernel(te_ref, nu_ref, xs_ref, wg_ref, wu_ref, wd_ref, y_ref, acc_sc, xb_sc):
    j = pl.program_id(0)
    f = pl.program_id(1)
    used = j < nu_ref[0]
    last = f == pl.num_programs(1) - 1

    @pl.when(jnp.logical_and(used, f == 0))
    def _():
        acc_sc[...] = jnp.zeros(acc_sc.shape, F32)
        xb_sc[...] = xs_ref[...].astype(BF16)

    @pl.when(used)
    def _():
        acc_sc[...] += _dot(_swiglu_mid(xb_sc[...], wg_ref, wu_ref), wd_ref[...])

    @pl.when(jnp.logical_and(used, last))
    def _():
        y_ref[...] = acc_sc[...]

    @pl.when(jnp.logical_and(jnp.logical_not(used), last))
    def _():
        y_ref[...] = jnp.zeros(y_ref.shape, F32)


def _moe_ffn(tile_expert, n_used, xs, wg, wu, wd, layer, cfg):
    P, D = xs.shape
    DFE = wd.shape[2]
    TME, tf = cfg["TME"], cfg["TFE"]
    nf = DFE // tf

    def jj(j, nu):
        return jnp.minimum(j, nu[0] - 1)

    def ff(j, f, nu):
        return jnp.where(j < nu[0], f, nf - 1)

    return pl.pallas_call(
        _moe_kernel,
        grid_spec=pltpu.PrefetchScalarGridSpec(
            num_scalar_prefetch=2,
            grid=(P // TME, nf),
            in_specs=[pl.BlockSpec((TME, D), lambda j, f, te, nu: (jj(j, nu), 0)),
                      pl.BlockSpec((None, None, D, tf),
                                   lambda j, f, te, nu: (layer, te[jj(j, nu)], 0, ff(j, f, nu))),
                      pl.BlockSpec((None, None, D, tf),
                                   lambda j, f, te, nu: (layer, te[jj(j, nu)], 0, ff(j, f, nu))),
                      pl.BlockSpec((None, None, tf, D),
                                   lambda j, f, te, nu: (layer, te[jj(j, nu)], ff(j, f, nu), 0))],
            out_specs=pl.BlockSpec((TME, D), lambda j, f, te, nu: (j, 0)),
            scratch_shapes=[pltpu.VMEM((TME, D), F32), pltpu.VMEM((TME, D), BF16)]),
        out_shape=jax.ShapeDtypeStruct((P, D), F32),
        compiler_params=_cp(("arbitrary", "arbitrary")),
        name="moe_expert_ffn",
    )(tile_expert, n_used, xs, wg, wu, wd)


def _combine_kernel(dest_ref, y_ref, wts_ref, x1_ref, mod_ref, gpost_ref, gnext_ref, modn_ref,
                    x2_ref, hn_ref, ybuf, sem, *, T, TMC, D):
    i = pl.program_id(0)
    slot = i % 2

    def gather(tile, dst_slot):
        def row_copy(r, k):
            return pltpu.make_async_copy(y_ref.at[pl.ds(dest_ref[k * T + tile * TMC + r], 1)],
                                         ybuf.at[dst_slot, k, pl.ds(r, 1)], sem.at[dst_slot])
        return row_copy

    @pl.when(i == 0)
    def _():
        _row_dma_start(gather(0, 0), TMC)

    @pl.when(i + 1 < pl.num_programs(0))
    def _():
        _row_dma_start(gather(i + 1, 1 - slot), TMC)

    _row_dma_wait(gather(i, slot), TMC)
    w = wts_ref[...]
    y = w[:, 0:1] * ybuf[slot, 0] + w[:, 1:2] * ybuf[slot, 1]
    _ffn_epilogue(y, x1_ref, mod_ref, gpost_ref, gnext_ref, modn_ref, x2_ref, hn_ref, D)


def _combine(dest_flat, y, wts, x1, mod, gpost, gnext, modn, cfg):
    T, D = x1.shape
    TMC = cfg["TM_MERGE"]
    row = lambda i, d: (i, 0)
    const = lambda i, d: (0, 0)
    modrow = lambda i, d: (cfg["mod_row"](i, TMC), 0, 0)
    return pl.pallas_call(
        functools.partial(_combine_kernel, T=T, TMC=TMC, D=D),
        grid_spec=pltpu.PrefetchScalarGridSpec(
            num_scalar_prefetch=1,
            grid=(T // TMC,),
            in_specs=[pl.BlockSpec(memory_space=pl.ANY),
                      pl.BlockSpec((TMC, TOP_K), row),
                      pl.BlockSpec((TMC, D), row),
                      pl.BlockSpec((1, 1, 6 * D), modrow),
                      pl.BlockSpec((1, D), const), pl.BlockSpec((1, D), const),
                      pl.BlockSpec((1, 1, 6 * D), modrow)],
            out_specs=[pl.BlockSpec((TMC, D), row), pl.BlockSpec((TMC, D), row)],
            scratch_shapes=[pltpu.VMEM((2, TOP_K, TMC, D), F32), pltpu.SemaphoreType.DMA((2,))]),
        out_shape=[jax.ShapeDtypeStruct((T, D), F32), jax.ShapeDtypeStruct((T, D), BF16)],
        compiler_params=_cp(("arbitrary",)),
        name="moe_combine",
    )(dest_flat, y, wts, x1, mod, gpost, gnext, modn)


def _rope_tables(Ss, TM):
    n_freq = LANES // 4
    t = jnp.arange(Ss, dtype=I32)
    rows = (t // GRID_W).astype(F32)
    cols = (t % GRID_W).astype(F32)
    inv_freq = jnp.power(ROPE_THETA, -jnp.arange(n_freq, dtype=F32) / n_freq)
    ar = rows[:, None] * inv_freq
    ac = cols[:, None] * inv_freq
    cos = jnp.concatenate([jnp.cos(ar), jnp.cos(ar), jnp.cos(ac), jnp.cos(ac)], axis=-1)
    sin = jnp.concatenate([-jnp.sin(ar), jnp.sin(ar), -jnp.sin(ac), jnp.sin(ac)], axis=-1)
    cos = jnp.concatenate([jnp.ones((TM, LANES), F32), cos], axis=0)
    sin = jnp.concatenate([jnp.zeros((TM, LANES), F32), sin], axis=0)
    return cos, sin


def _dft_tables(n):
    j = jnp.arange(n, dtype=I32)
    ph = (j[:, None] * j[None, :]) % n
    ang = ph.astype(F32) * (2.0 * math.pi / n)
    return jnp.cos(ang).astype(BF16), jnp.sin(ang).astype(BF16)


def kernel(x_prompt, x_sample, cache_k, cache_v, c, c_ctx, w_mod, b_mod, g_pre_mix, g_post_mix,
           g_pre_ffn, g_post_ffn, w_in, g_q, g_k, w_pool_map, pool_scale, w_attn_o, w_pool_o,
           w_four_o, w_out, w_ffn_gate, w_ffn_up, w_ffn_down, w_router, w_exp_gate, w_exp_up,
           w_exp_down):
    Bp, Sp, D = x_prompt.shape
    Bs, Ss, _ = x_sample.shape
    L = w_mod.shape[0]
    PAST, KV = cache_k.shape[2], cache_k.shape[3]
    AW, PW, FW = w_attn_o.shape[1], w_pool_o.shape[1], w_four_o.shape[1]
    NH = AW // LANES
    G = NH // KV
    KW = KV * LANES
    E = w_router.shape[2]
    NP, NS = Bp * Sp, Bs * Ss
    T = NP + NS
    TM = min(512, math.gcd(NP, Ss))
    assert cache_k.shape[4] == LANES and NP % TM == 0 and Ss % TM == 0 and NP % Ss == 0
    assert Ss % Sp == 0 and Sp & (Sp - 1) == 0 and Ss & (Ss - 1) == 0 and Ss % GRID_W == 0
    NPT, TPS = NP // TM, Ss // TM
    TM_BIG = min(1024, math.gcd(NP, Ss))
    TME = TM
    P = (-(-(T * TOP_K) // TME) + E) * TME

    pick_tile = lambda n, cands: next(t for t in cands if n % t == 0)
    cfg = dict(
        TM=TM, TM_MERGE=min(TM, 256), TME=TME, TM_FFN=TM, TM_GATES=TM_BIG,
        TF=pick_tile(w_ffn_gate.shape[2], (512, 256, 128)),
        TFE=pick_tile(w_exp_gate.shape[3], (1024, 512, 256, 128)),
        NH=NH, KV=KV, PW=PW, FW=FW, NP=NP, Sp=Sp, Ss=Ss,
        mod_row=lambda i, tm=TM: jnp.where(i * tm < NP, 0, 1 + (i * tm - NP) // Ss),
        rope_blk=lambda i: jnp.where(i < NPT, 0, 1 + (i - NPT) % TPS),
    )

    x = jnp.concatenate([x_prompt.reshape(NP, D), x_sample.reshape(NS, D)], axis=0)
    R = -(-(1 + Bs) // 8) * 8
    cond = jnp.zeros((R, D), F32).at[0].set(c_ctx).at[1:1 + Bs].set(c)
    mod_all = _modulation(cond, w_mod, b_mod).reshape(L, R, 1, 6 * D)

    o1, o2, o3, o4 = AW, AW + 2 * KW, AW + 2 * KW + PW, AW + 2 * KW + PW + FW
    w_in_b = w_in.astype(BF16)
    cos_tab, sin_tab = _rope_tables(Ss, TM)
    cc, sc = _dft_tables(LANES)
    cs_chan = jnp.concatenate([cc, sc], axis=1)
    dft_p = _dft_tables(Sp)
    dft_s = _dft_tables(Ss)
    ck = cache_k.astype(BF16).reshape(Bs, L, PAST, KW)
    cv = cache_v.astype(BF16).reshape(Bs, L, PAST, KW)
    row1 = lambda a: a.reshape(1, -1)
    dense_w = [w.astype(BF16) for w in (w_ffn_gate, w_ffn_up, w_ffn_down)]
    expert_w = [w.astype(BF16) for w in (w_exp_gate, w_exp_up, w_exp_down)]

    h = _norm_mod(x, row1(g_pre_mix[0]), mod_all[0], cfg)
    new_k, new_v = [], []
    for l in range(L):
        mod = mod_all[l]
        q, kb, vb, kf, vf = _qkv(h, w_in_b[l, :, :o2], row1(g_q[l]), row1(g_k[l]), cos_tab, sin_tab, cfg)
        u_pool, fa, fb = _pool_fourier_proj(h, w_in_b[l, :, o2:o4], cs_chan, cfg)
        gates = _gates(h, w_in_b[l, :, o4:], cfg)
        new_k.append(kf[:NP].reshape(Bp, Sp, KV, LANES))
        new_v.append(vf[:NP].reshape(Bp, Sp, KV, LANES))

        attn_p = _attention(q, kb, vb, q_row0=0, B=Bp, S=Sp, Tk=Sp, KV=KV, G=G)
        k_s = jnp.concatenate([ck[:, l], kb[NP:].reshape(Bs, Ss, KW)], axis=1).reshape(-1, KW)
        v_s = jnp.concatenate([cv[:, l], vb[NP:].reshape(Bs, Ss, KW)], axis=1).reshape(-1, KW)
        attn_s = _attention(q, k_s, v_s, q_row0=NP, B=Bs, S=Ss, Tk=PAST + Ss, KV=KV, G=G)

        pool = _pool(u_pool, w_pool_map[l].astype(BF16), row1(pool_scale[l]), cfg)
        four_p = _fourier(fa, fb, *dft_p, row0=0, B=Bp, S=Sp)
        four_s = _fourier(fa, fb, *dft_s, row0=NP, B=Bs, S=Ss)

        x1, h2 = _merge(attn_p, attn_s, pool, four_p, four_s, gates, x, mod, w_attn_o[l].astype(BF16),
                        w_pool_o[l].astype(BF16), w_four_o[l].astype(BF16), w_out[l].astype(BF16),
                        row1(g_post_mix[l]), row1(g_pre_ffn[l]), BF16 if l % 2 == 0 else F32, cfg)

        ln = min(l + 1, L - 1)
        gnext, modn = row1(g_pre_mix[ln]), mod_all[ln]
        i = l // 2
        if l % 2 == 0:
            x, h = _dense_ffn(h2, *dense_w, i, x1, mod, row1(g_post_ffn[l]), gnext, modn, cfg)
        else:
            eidx, rank, wts, cnt = _router(h2, w_router[i].T, cfg)
            counts = cnt[:, 0]
            padded = (counts + TME - 1) // TME * TME
            pad_end = jnp.cumsum(padded)
            pad_start = pad_end - padded
            start_of = sum(jnp.where(eidx == e, pad_start[e], 0) for e in range(E))
            dest = (start_of + rank).reshape(-1).astype(I32)
            n_used = (pad_end[-1] // TME).astype(I32).reshape(1)
            tile_start = jnp.arange(P // TME, dtype=I32) * TME
            tile_expert = jnp.minimum(
                jnp.sum(tile_start[:, None] >= pad_end[None, :], axis=1), E - 1).astype(I32)
            pad_lo = jnp.concatenate([pad_start + counts, pad_end[-1:]]).astype(I32)
            pad_hi = jnp.concatenate([pad_end, jnp.full((1,), P, pad_end.dtype)]).astype(I32)
            xs = _dispatch(dest, pad_lo, pad_hi, h2, P, cfg)
            y = _moe_ffn(tile_expert, n_used, xs, *expert_w, i, cfg)
            x, h = _combine(dest, y, wts.T, x1, mod, row1(g_post_ffn[l]), gnext, modn, cfg)

    y_prompt = x[:NP].reshape(Bp, Sp, D)
    y_sample = x[NP:].reshape(Bs, Ss, D)
    return (y_prompt, y_sample, jnp.stack(new_k, axis=1), jnp.stack(new_v, axis=1))
```

```python
import functools
import math

import jax
import jax.numpy as jnp
from jax import lax
from jax.experimental import pallas as pl
from jax.experimental.pallas import tpu as pltpu

F32 = jnp.float32
BF16 = jnp.bfloat16
I32 = jnp.int32

LANES = 128
GRID_W = 64
ROPE_THETA = 10000.0
EPS = 1e-6
TOP_K = 2
POOL_WINDOWS = (2, 4, 8, 16)
VMEM_LIMIT = 56 * 1024 * 1024


def _cp(sem, vmem=VMEM_LIMIT):
    return pltpu.CompilerParams(dimension_semantics=sem, vmem_limit_bytes=vmem)


def _rms(x, g):
    return x * lax.rsqrt(jnp.mean(x * x, axis=-1, keepdims=True) + EPS) * g


def _silu(x):
    return x / (1.0 + jnp.exp(-x))


def _dot(a, b):
    return jnp.dot(a, b, preferred_element_type=F32)


def _resident(shape, index_map):
    return pl.BlockSpec(shape, index_map, pipeline_mode=pl.Buffered(1))


def _mod_kernel(c_ref, w_ref, b_ref, o_ref):
    s = _silu(c_ref[...])
    o_ref[0] = jnp.dot(s, w_ref[0], preferred_element_type=F32,
                       precision=lax.Precision.HIGHEST) + b_ref[0]


def _modulation(cond, w_mod, b_mod):
    L, D, N = w_mod.shape
    R = cond.shape[0]
    tn = 1024
    return pl.pallas_call(
        _mod_kernel,
        grid=(L, N // tn),
        in_specs=[pl.BlockSpec((R, D), lambda l, j: (0, 0)),
                  pl.BlockSpec((1, D, tn), lambda l, j: (l, 0, j)),
                  pl.BlockSpec((1, 1, tn), lambda l, j: (l, 0, j))],
        out_specs=pl.BlockSpec((1, R, tn), lambda l, j: (l, 0, j)),
        out_shape=jax.ShapeDtypeStruct((L, R, N), F32),
        compiler_params=_cp(("arbitrary", "arbitrary")),
        name="modulation",
    )(cond, w_mod, b_mod.reshape(L, 1, N))


def _norm_mod_kernel(x_ref, g_ref, mod_ref, h_ref, *, D):
    mod = mod_ref[0]
    y = _rms(x_ref[...], g_ref[...])
    h_ref[...] = (y * (1.0 + mod[:, D:2 * D]) + mod[:, 0:D]).astype(BF16)


def _norm_mod(x, g, mod, cfg):
    T, D = x.shape
    TM = cfg["TM"]
    return pl.pallas_call(
        functools.partial(_norm_mod_kernel, D=D),
        grid=(T // TM,),
        in_specs=[pl.BlockSpec((TM, D), lambda i: (i, 0)),
                  pl.BlockSpec((1, D), lambda i: (0, 0)),
                  pl.BlockSpec((1, 1, 6 * D), lambda i: (cfg["mod_row"](i), 0, 0))],
        out_specs=pl.BlockSpec((TM, D), lambda i: (i, 0)),
        out_shape=jax.ShapeDtypeStruct((T, D), BF16),
        compiler_params=_cp(("arbitrary",)),
        name="norm_mod",
    )(x, g, mod)


def _qkv_kernel(h_ref, w_ref, gq_ref, gk_ref, cos_ref, sin_ref,
                q_ref, kb_ref, vb_ref, kf_ref, vf_ref, *, NH, KV, scale):
    acc = _dot(h_ref[...], w_ref[...])
    cos = cos_ref[...]
    sin = sin_ref[...]
    lane = lax.broadcasted_iota(I32, cos.shape, 1)
    first_half = (lane % (LANES // 2)) < (LANES // 4)

    def rope(y):
        partner = jnp.where(first_half, pltpu.roll(y, LANES - LANES // 4, axis=1),
                            pltpu.roll(y, LANES // 4, axis=1))
        return y * cos + partner * sin

    for hd in range(NH):
        z = acc[:, hd * LANES:(hd + 1) * LANES]
        q_ref[:, hd * LANES:(hd + 1) * LANES] = (rope(_rms(z, gq_ref[...])) * scale).astype(BF16)
    for hd in range(KV):
        z = acc[:, (NH + hd) * LANES:(NH + hd + 1) * LANES]
        y = _rms(z, gk_ref[...])
        kf_ref[:, hd * LANES:(hd + 1) * LANES] = y
        kb_ref[:, hd * LANES:(hd + 1) * LANES] = rope(y).astype(BF16)
    v = acc[:, (NH + KV) * LANES:(NH + 2 * KV) * LANES]
    vf_ref[...] = v
    vb_ref[...] = v.astype(BF16)


def _qkv(h, w, gq, gk, cos_tab, sin_tab, cfg):
    T, D = h.shape
    TM, NH, KV = cfg["TM"], cfg["NH"], cfg["KV"]
    AW, KW = NH * LANES, KV * LANES
    row = lambda i: (i, 0)
    return pl.pallas_call(
        functools.partial(_qkv_kernel, NH=NH, KV=KV, scale=float(LANES) ** -0.5 * math.log2(math.e)),
        grid=(T // TM,),
        in_specs=[pl.BlockSpec((TM, D), row),
                  _resident((D, AW + 2 * KW), lambda i: (0, 0)),
                  pl.BlockSpec((1, LANES), lambda i: (0, 0)),
                  pl.BlockSpec((1, LANES), lambda i: (0, 0)),
                  pl.BlockSpec((TM, LANES), lambda i: (cfg["rope_blk"](i), 0)),
                  pl.BlockSpec((TM, LANES), lambda i: (cfg["rope_blk"](i), 0))],
        out_specs=[pl.BlockSpec((TM, AW), row), pl.BlockSpec((TM, KW), row),
                   pl.BlockSpec((TM, KW), row), pl.BlockSpec((TM, KW), row),
                   pl.BlockSpec((TM, KW), row)],
        out_shape=[jax.ShapeDtypeStruct((T, AW), BF16), jax.ShapeDtypeStruct((T, KW), BF16),
                   jax.ShapeDtypeStruct((T, KW), BF16), jax.ShapeDtypeStruct((T, KW), F32),
                   jax.ShapeDtypeStruct((T, KW), F32)],
        compiler_params=_cp(("arbitrary",)),
        name="qkv_proj",
    )(h, w, gq, gk, cos_tab, sin_tab)


def _pf_kernel(h_ref, w_ref, cs_ref, up_ref, a_ref, b_ref, *, PW, NG):
    acc = _dot(h_ref[...], w_ref[...])
    up_ref[...] = acc[:, :PW]
    for g in range(NG):
        u = acc[:, PW + g * LANES:PW + (g + 1) * LANES].astype(BF16)
        ab = _dot(u, cs_ref[...])
        a_ref[:, g * LANES:(g + 1) * LANES] = ab[:, :LANES].astype(BF16)
        b_ref[:, g * LANES:(g + 1) * LANES] = ab[:, LANES:].astype(BF16)


def _pool_fourier_proj(h, w, cs, cfg):
    T, D = h.shape
    TM, PW, FW = cfg["TM"], cfg["PW"], cfg["FW"]
    row = lambda i: (i, 0)
    return pl.pallas_call(
        functools.partial(_pf_kernel, PW=PW, NG=FW // LANES),
        grid=(T // TM,),
        in_specs=[pl.BlockSpec((TM, D), row),
                  _resident((D, PW + FW), lambda i: (0, 0)),
                  pl.BlockSpec((LANES, 2 * LANES), lambda i: (0, 0))],
        out_specs=[pl.BlockSpec((TM, PW), row), pl.BlockSpec((TM, FW), row),
                   pl.BlockSpec((TM, FW), row)],
        out_shape=[jax.ShapeDtypeStruct((T, PW), F32), jax.ShapeDtypeStruct((T, FW), BF16),
                   jax.ShapeDtypeStruct((T, FW), BF16)],
        compiler_params=_cp(("arbitrary",)),
        name="pool_fourier_proj",
    )(h, w, cs)


def _gates_kernel(h_ref, w_ref, o_ref):
    z = _dot(h_ref[...], w_ref[...])
    o_ref[...] = (1.0 / (1.0 + jnp.exp(-z))).astype(BF16)


def _gates(h, w, cfg):
    T, D = h.shape
    N = w.shape[1]
    TM = cfg["TM_GATES"]
    tn = 2048 if N % 2048 == 0 else 1024
    return pl.pallas_call(
        _gates_kernel,
        grid=(N // tn, T // TM),
        in_specs=[pl.BlockSpec((TM, D), lambda j, i: (i, 0)),
                  pl.BlockSpec((D, tn), lambda j, i: (0, j))],
        out_specs=pl.BlockSpec((TM, tn), lambda j, i: (i, j)),
        out_shape=jax.ShapeDtypeStruct((T, N), BF16),
        compiler_params=_cp(("arbitrary", "arbitrary")),
        name="gates_proj",
    )(h, w)


def _attn_kernel(q_ref, k_ref, v_ref, o_ref, *, G, tq, tk, nk):
    q = jnp.concatenate([q_ref[:, g * LANES:(g + 1) * LANES] for g in range(G)], axis=0)
    rows = G * tq
    nt = tk // LANES
    m = jnp.full((rows, LANES), -jnp.inf, F32)
    l = jnp.zeros((rows, LANES), F32)
    acc = jnp.zeros((rows, LANES), F32)
    for c in range(nk):
        k = k_ref[c * tk:(c + 1) * tk, :]
        v = v_ref[c * tk:(c + 1) * tk, :]
        s = lax.dot_general(q, k, (((1,), (1,)), ((), ())), preferred_element_type=F32)
        m_new = jnp.maximum(m, jnp.max(s, axis=-1, keepdims=True))
        alpha = jnp.exp2(m - m_new)
        p = [jnp.exp2(s[:, t * LANES:(t + 1) * LANES] - m_new) for t in range(nt)]
        l = alpha * l + functools.reduce(lambda a, b: a + b, p)
        pb = jnp.concatenate([pt.astype(BF16) for pt in p], axis=1)
        acc = alpha * acc + _dot(pb, v)
        m = m_new
    o = acc / jnp.sum(l, axis=-1, keepdims=True)
    for g in range(G):
        o_ref[:, g * LANES:(g + 1) * LANES] = o[g * tq:(g + 1) * tq].astype(BF16)


def _attention(q, k, v, *, q_row0, B, S, Tk, KV, G):
    tq = min(256, S)
    tk = next(t for t in (512, 256, 128) if Tk % t == 0)
    nq, nk = S // tq, Tk // tk
    qb0 = q_row0 // tq
    return pl.pallas_call(
        functools.partial(_attn_kernel, G=G, tq=tq, tk=tk, nk=nk),
        grid=(B, KV, nq),
        in_specs=[pl.BlockSpec((tq, G * LANES), lambda b, h, qi: (qb0 + b * nq + qi, h)),
                  pl.BlockSpec((Tk, LANES), lambda b, h, qi: (b, h)),
                  pl.BlockSpec((Tk, LANES), lambda b, h, qi: (b, h))],
        out_specs=pl.BlockSpec((tq, G * LANES), lambda b, h, qi: (b * nq + qi, h)),
        out_shape=jax.ShapeDtypeStruct((B * S, KV * G * LANES), BF16),
        compiler_params=_cp(("arbitrary",) * 3),
        name="attention",
    )(q, k, v)


def _pool_kernel(u_ref, wmap_ref, scale_ref, o_ref, tot_sc, *, R, n_prompt_blocks, Sp, Ss):
    blk = pl.program_id(0)
    g = pl.program_id(1)
    S = jnp.where(blk < n_prompt_blocks, Sp, Ss)
    half = jnp.left_shift(1, g)
    pos = lax.broadcasted_iota(I32, (R, LANES), 0) & (S - 1)

    def add_offsets(offsets):
        u = u_ref[...]
        tot = tot_sc[...]
        for k in offsets:
            valid = jnp.logical_and(pos + k >= 0, pos + k < S)
            tot = tot + jnp.where(valid, pltpu.roll(u, (-k) % R, axis=0), 0.0)
        tot_sc[...] = tot

    tot_sc[...] = u_ref[...]
    add_offsets([-1])
    for lvl in range(1, len(POOL_WINDOWS)):
        h0, h1 = POOL_WINDOWS[lvl - 1] // 2, POOL_WINDOWS[lvl] // 2

        @pl.when(g >= lvl)
        def _():
            add_offsets(list(range(-h1, -h0)) + list(range(h0, h1)))

    cnt = (jnp.minimum(pos + half, S) - jnp.maximum(pos - half, 0)).astype(F32)
    pooled = tot_sc[...] / cnt - u_ref[...]
    mixed = _dot(pooled.astype(BF16), wmap_ref[0]) * scale_ref[...]
    o_ref[...] = mixed.astype(BF16)


def _pool(u, wmap, scale, cfg):
    T, PW = u.shape
    R, NP, Sp, Ss = cfg["Ss"], cfg["NP"], cfg["Sp"], cfg["Ss"]
    return pl.pallas_call(
        functools.partial(_pool_kernel, R=R, n_prompt_blocks=NP // R, Sp=Sp, Ss=Ss),
        grid=(T // R, PW // LANES),
        in_specs=[pl.BlockSpec((R, LANES), lambda b, g: (b, g)),
                  pl.BlockSpec((1, LANES, LANES), lambda b, g: (g, 0, 0)),
                  pl.BlockSpec((1, LANES), lambda b, g: (0, g))],
        out_specs=pl.BlockSpec((R, LANES), lambda b, g: (b, g)),
        out_shape=jax.ShapeDtypeStruct((T, PW), BF16),
        scratch_shapes=[pltpu.VMEM((R, LANES), F32)],
        compiler_params=_cp(("arbitrary", "arbitrary")),
        name="multiscale_pool",
    )(u, wmap, scale)


def _fourier_kernel(cs_ref, ss_ref, a_ref, b_ref, o_ref, acc_sc, *, scale):
    k = pl.program_id(2)

    @pl.when(k == 0)
    def _():
        acc_sc[...] = jnp.zeros(acc_sc.shape, F32)

    acc_sc[...] += _dot(cs_ref[...], a_ref[...]) - _dot(ss_ref[...], b_ref[...])

    @pl.when(k == pl.num_programs(2) - 1)
    def _():
        o_ref[...] = (acc_sc[...] * scale).astype(BF16)


def _fourier(a, b, cs, ss, *, row0, B, S):
    FW = a.shape[1]
    tm = min(1024, S)
    tk = min(1024, S)
    nm, nk = S // tm, S // tk
    kb0 = row0 // tk
    return pl.pallas_call(
        functools.partial(_fourier_kernel, scale=float(S * LANES) ** -0.5),
        grid=(B, nm, nk),
        in_specs=[pl.BlockSpec((tm, tk), lambda bb, i, k: (i, k)),
                  pl.BlockSpec((tm, tk), lambda bb, i, k: (i, k)),
                  pl.BlockSpec((tk, FW), lambda bb, i, k: (kb0 + bb * nk + k, 0)),
                  pl.BlockSpec((tk, FW), lambda bb, i, k: (kb0 + bb * nk + k, 0))],
        out_specs=pl.BlockSpec((tm, FW), lambda bb, i, k: (bb * nm + i, 0)),
        out_shape=jax.ShapeDtypeStruct((B * S, FW), BF16),
        scratch_shapes=[pltpu.VMEM((tm, FW), F32)],
        compiler_params=_cp(("arbitrary",) * 3),
        name="fourier_mix",
    )(cs, ss, a, b)


def _merge_kernel(attn_p_ref, attn_s_ref, pool_ref, four_p_ref, four_s_ref, gates_ref, x_ref, mod_ref,
                  wa_ref, wp_ref, wf_ref, wo_ref, gpost_ref, gpre_ref, x1_ref, h2_ref, *, D, n_prompt_tiles):
    mod = mod_ref[0]
    is_prompt = pl.program_id(0) < n_prompt_tiles
    attn = jnp.where(is_prompt, attn_p_ref[...], attn_s_ref[...])
    four = jnp.where(is_prompt, four_p_ref[...], four_s_ref[...])
    merged = (gates_ref[:, 0:D].astype(F32) * _dot(attn, wa_ref[...])
              + gates_ref[:, D:2 * D].astype(F32) * _dot(pool_ref[...], wp_ref[...])
              + gates_ref[:, 2 * D:3 * D].astype(F32) * _dot(four, wf_ref[...]))
    y = _dot(merged.astype(BF16), wo_ref[...])
    x1 = x_ref[...] + mod[:, 2 * D:3 * D] * _rms(y, gpost_ref[...])
    x1_ref[...] = x1
    h2 = _rms(x1, gpre_ref[...]) * (1.0 + mod[:, 4 * D:5 * D]) + mod[:, 3 * D:4 * D]
    h2_ref[...] = h2.astype(h2_ref.dtype)


def _merge(attn_p, attn_s, pool, four_p, four_s, gates, x, mod, wa, wp, wf, wo, gpost, gpre, h2_dtype, cfg):
    T, D = x.shape
    TM = cfg["TM_MERGE"]
    npt = cfg["NP"] // TM
    row = lambda i: (i, 0)
    const = lambda i: (0, 0)
    prow = lambda i: (jnp.minimum(i, npt - 1), 0)
    srow = lambda i: (jnp.maximum(i - npt, 0), 0)
    AW, FW = attn_p.shape[1], four_p.shape[1]
    return pl.pallas_call(
        functools.partial(_merge_kernel, D=D, n_prompt_tiles=npt),
        grid=(T // TM,),
        in_specs=[pl.BlockSpec((TM, AW), prow), pl.BlockSpec((TM, AW), srow),
                  pl.BlockSpec((TM, pool.shape[1]), row),
                  pl.BlockSpec((TM, FW), prow), pl.BlockSpec((TM, FW), srow),
                  pl.BlockSpec((TM, 3 * D), row),
                  pl.BlockSpec((TM, D), row),
                  pl.BlockSpec((1, 1, 6 * D), lambda i: (cfg["mod_row"](i, TM), 0, 0)),
                  _resident(wa.shape, const), _resident(wp.shape, const),
                  _resident(wf.shape, const), _resident(wo.shape, const),
                  pl.BlockSpec((1, D), const), pl.BlockSpec((1, D), const)],
        out_specs=[pl.BlockSpec((TM, D), row), pl.BlockSpec((TM, D), row)],
        out_shape=[jax.ShapeDtypeStruct((T, D), F32), jax.ShapeDtypeStruct((T, D), h2_dtype)],
        compiler_params=_cp(("arbitrary",)),
        name="merge_out_proj",
    )(attn_p, attn_s, pool, four_p, four_s, gates, x, mod, wa, wp, wf, wo, gpost, gpre)


def _ffn_epilogue(y, x1_ref, mod_ref, gpost_ref, gnext_ref, modn_ref, x2_ref, hn_ref, D):
    mod = mod_ref[0]
    x2 = x1_ref[...] + mod[:, 5 * D:6 * D] * _rms(y, gpost_ref[...])
    x2_ref[...] = x2
    modn = modn_ref[0]
    hn = _rms(x2, gnext_ref[...]) * (1.0 + modn[:, D:2 * D]) + modn[:, 0:D]
    hn_ref[...] = hn.astype(BF16)


def _swiglu_mid(x, wg_ref, wu_ref):
    return (_silu(_dot(x, wg_ref[...])) * _dot(x, wu_ref[...])).astype(BF16)


def _ffn_kernel(h_ref, wg_ref, wu_ref, wd_ref, x1_ref, mod_ref, gpost_ref, gnext_ref, modn_ref,
                x2_ref, hn_ref, acc_sc, *, D):
    f = pl.program_id(1)

    @pl.when(f == 0)
    def _():
        acc_sc[...] = jnp.zeros(acc_sc.shape, F32)

    acc_sc[...] += _dot(_swiglu_mid(h_ref[...], wg_ref, wu_ref), wd_ref[...])

    @pl.when(f == pl.num_programs(1) - 1)
    def _():
        _ffn_epilogue(acc_sc[...], x1_ref, mod_ref, gpost_ref, gnext_ref, modn_ref, x2_ref, hn_ref, D)


def _dense_ffn(h, wg, wu, wd, layer, x1, mod, gpost, gnext, modn, cfg):
    T, D = h.shape
    DFF = wd.shape[1]
    TM = cfg["TM_FFN"]
    tf = cfg["TF"]
    row = lambda i, f: (i, 0)
    const = lambda i, f: (0, 0)
    modrow = lambda i, f: (cfg["mod_row"](i, TM), 0, 0)
    return pl.pallas_call(
        functools.partial(_ffn_kernel, D=D),
        grid=(T // TM, DFF // tf),
        in_specs=[pl.BlockSpec((TM, D), row),
                  pl.BlockSpec((None, D, tf), lambda i, f: (layer, 0, f)),
                  pl.BlockSpec((None, D, tf), lambda i, f: (layer, 0, f)),
                  pl.BlockSpec((None, tf, D), lambda i, f: (layer, f, 0)),
                  pl.BlockSpec((TM, D), row),
                  pl.BlockSpec((1, 1, 6 * D), modrow),
                  pl.BlockSpec((1, D), const), pl.BlockSpec((1, D), const),
                  pl.BlockSpec((1, 1, 6 * D), modrow)],
        out_specs=[pl.BlockSpec((TM, D), row), pl.BlockSpec((TM, D), row)],
        out_shape=[jax.ShapeDtypeStruct((T, D), F32), jax.ShapeDtypeStruct((T, D), BF16)],
        scratch_shapes=[pltpu.VMEM((TM, D), F32)],
        compiler_params=_cp(("arbitrary", "arbitrary")),
        name="dense_ffn",
    )(h, wg, wu, wd, x1, mod, gpost, gnext, modn)


def _router_kernel(h_ref, wr_ref, eidx_ref, rank_ref, wts_ref, cnt_ref, base_sc, *, E, TMR):
    i = pl.program_id(0)

    @pl.when(i == 0)
    def _():
        base_sc[...] = jnp.zeros(base_sc.shape, F32)

    logits = lax.dot_general(wr_ref[...], h_ref[...].astype(F32), (((1,), (1,)), ((), ())),
                             preferred_element_type=F32, precision=lax.Precision.HIGHEST)
    e_iota = lax.broadcasted_iota(I32, (E, TMR), 0)
    m1 = jnp.max(logits, axis=0, keepdims=True)
    i1 = jnp.min(jnp.where(logits == m1, e_iota, E), axis=0, keepdims=True)
    rest = jnp.where(e_iota == i1, -jnp.inf, logits)
    m2 = jnp.max(rest, axis=0, keepdims=True)
    i2 = jnp.min(jnp.where(rest == m2, e_iota, E), axis=0, keepdims=True)
    ex = jnp.exp(m2 - m1)
    w1 = 1.0 / (1.0 + ex)
    w2 = ex / (1.0 + ex)

    oh1 = e_iota == i1
    oh2 = e_iota == i2
    oh = jnp.where(jnp.logical_or(oh1, oh2), 1.0, 0.0)
    tri = jnp.where(lax.broadcasted_iota(I32, (TMR, TMR), 0) < lax.broadcasted_iota(I32, (TMR, TMR), 1),
                    1.0, 0.0).astype(BF16)
    before = _dot(oh.astype(BF16), tri) + base_sc[...]
    r1 = jnp.sum(jnp.where(oh1, before, 0.0), axis=0, keepdims=True)
    r2 = jnp.sum(jnp.where(oh2, before, 0.0), axis=0, keepdims=True)
    base_sc[...] = base_sc[...] + jnp.sum(oh, axis=1, keepdims=True)

    eidx_ref[0:1, :] = i1
    eidx_ref[1:2, :] = i2
    rank_ref[0:1, :] = r1.astype(I32)
    rank_ref[1:2, :] = r2.astype(I32)
    wts_ref[0:1, :] = w1
    wts_ref[1:2, :] = w2
    cnt_ref[...] = jnp.broadcast_to(base_sc[...], cnt_ref.shape).astype(I32)


def _router(h, wr_t, cfg):
    T, D = h.shape
    E = wr_t.shape[0]
    TMR = cfg["TM"]
    col = lambda i: (0, i)
    return pl.pallas_call(
        functools.partial(_router_kernel, E=E, TMR=TMR),
        grid=(T // TMR,),
        in_specs=[pl.BlockSpec((TMR, D), lambda i: (i, 0)),
                  pl.BlockSpec((E, D), lambda i: (0, 0))],
        out_specs=[pl.BlockSpec((TOP_K, TMR), col), pl.BlockSpec((TOP_K, TMR), col),
                   pl.BlockSpec((TOP_K, TMR), col), pl.BlockSpec((E, LANES), lambda i: (0, 0))],
        out_shape=[jax.ShapeDtypeStruct((TOP_K, T), I32), jax.ShapeDtypeStruct((TOP_K, T), I32),
                   jax.ShapeDtypeStruct((TOP_K, T), F32), jax.ShapeDtypeStruct((E, LANES), I32)],
        scratch_shapes=[pltpu.VMEM((E, 1), F32)],
        compiler_params=_cp(("arbitrary",)),
        name="moe_router",
    )(h, wr_t)


ROW_DMA_UNROLL = 8


def _row_dma_start(row_copy, n_rows):
    def start(rb, c):
        for u in range(ROW_DMA_UNROLL):
            for k in range(TOP_K):
                row_copy(rb * ROW_DMA_UNROLL + u, k).start(priority=(u * TOP_K + k) % 2)
        return c

    lax.fori_loop(0, n_rows // ROW_DMA_UNROLL, start, 0)


def _row_dma_wait(row_copy, n_rows):
    def wait(rb, c):
        for u in range(ROW_DMA_UNROLL):
            for k in range(TOP_K):
                row_copy(rb * ROW_DMA_UNROLL + u, k).wait()
        return c

    lax.fori_loop(0, n_rows // ROW_DMA_UNROLL, wait, 0)


def _dispatch_kernel(dest_ref, pad_lo_ref, pad_hi_ref, h_ref, xs_ref, zero_row, sem, pad_sem,
                     *, T, TMD, n_pad_ranges):
    base = pl.program_id(0) * TMD

    @pl.when(pl.program_id(0) == 0)
    def _():
        zero_row[...] = jnp.zeros(zero_row.shape, zero_row.dtype)

        def pad_copy(r):
            return pltpu.make_async_copy(zero_row, xs_ref.at[pl.ds(r, 1)], pad_sem)

        def start(r, c):
            pad_copy(r).start()
            return c

        def wait(r, c):
            pad_copy(r).wait()
            return c

        for z in range(n_pad_ranges):
            lax.fori_loop(pad_lo_ref[z], pad_hi_ref[z], start, 0)
        for z in range(n_pad_ranges):
            lax.fori_loop(pad_lo_ref[z], pad_hi_ref[z], wait, 0)

    def row_copy(r, k):
        return pltpu.make_async_copy(h_ref.at[pl.ds(r, 1)],
                                     xs_ref.at[pl.ds(dest_ref[k * T + base + r], 1)], sem)

    _row_dma_start(row_copy, TMD)
    _row_dma_wait(row_copy, TMD)


def _dispatch(dest_flat, pad_lo, pad_hi, h, P, cfg):
    T, D = h.shape
    TMD = cfg["TM"]
    return pl.pallas_call(
        functools.partial(_dispatch_kernel, T=T, TMD=TMD, n_pad_ranges=pad_lo.shape[0]),
        grid_spec=pltpu.PrefetchScalarGridSpec(
            num_scalar_prefetch=3,
            grid=(T // TMD,),
            in_specs=[pl.BlockSpec((TMD, D), lambda i, dest, lo, hi: (i, 0))],
            out_specs=pl.BlockSpec(memory_space=pl.ANY),
            scratch_shapes=[pltpu.VMEM((1, D), h.dtype), pltpu.SemaphoreType.DMA(()),
                            pltpu.SemaphoreType.DMA(())]),
        out_shape=jax.ShapeDtypeStruct((P, D), h.dtype),
        compiler_params=pltpu.CompilerParams(dimension_semantics=("arbitrary",),
                                             has_side_effects=True),
        name="moe_dispatch",
    )(dest_flat, pad_lo, pad_hi, h)


def _moe_kernel(te_ref, nu_ref, xs_ref, wg_ref, wu_ref, wd_ref, y_ref, acc_sc, xb_sc):
    j = pl.program_id(0)
    f = pl.program_id(1)
    used = j < nu_ref[0]
    last = f == pl.num_programs(1) - 1

    @pl.when(jnp.logical_and(used, f == 0))
    def _():
        acc_sc[...] = jnp.zeros(acc_sc.shape, F32)
        xb_sc[...] = xs_ref[...].astype(BF16)

    @pl.when(used)
    def _():
        acc_sc[...] += _dot(_swiglu_mid(xb_sc[...], wg_ref, wu_ref), wd_ref[...])

    @pl.when(jnp.logical_and(used, last))
    def _():
        y_ref[...] = acc_sc[...]

    @pl.when(jnp.logical_and(jnp.logical_not(used), last))
    def _():
        y_ref[...] = jnp.zeros(y_ref.shape, F32)


def _moe_ffn(tile_expert, n_used, xs, wg, wu, wd, layer, cfg):
    P, D = xs.shape
    DFE = wd.shape[2]
    TME, tf = cfg["TME"], cfg["TFE"]
    nf = DFE // tf

    def jj(j, nu):
        return jnp.minimum(j, nu[0] - 1)

    def ff(j, f, nu):
        return jnp.where(j < nu[0], f, nf - 1)

    return pl.pallas_call(
        _moe_kernel,
        grid_spec=pltpu.PrefetchScalarGridSpec(
            num_scalar_prefetch=2,
            grid=(P // TME, nf),
            in_specs=[pl.BlockSpec((TME, D), lambda j, f, te, nu: (jj(j, nu), 0)),
                      pl.BlockSpec((None, None, D, tf),
                                   lambda j, f, te, nu: (layer, te[jj(j, nu)], 0, ff(j, f, nu))),
                      pl.BlockSpec((None, None, D, tf),
                                   lambda j, f, te, nu: (layer, te[jj(j, nu)], 0, ff(j, f, nu))),
                      pl.BlockSpec((None, None, tf, D),
                                   lambda j, f, te, nu: (layer, te[jj(j, nu)], ff(j, f, nu), 0))],
            out_specs=pl.BlockSpec((TME, D), lambda j, f, te, nu: (j, 0)),
            scratch_shapes=[pltpu.VMEM((TME, D), F32), pltpu.VMEM((TME, D), BF16)]),
        out_shape=jax.ShapeDtypeStruct((P, D), F32),
        compiler_params=_cp(("arbitrary", "arbitrary")),
        name="moe_expert_ffn",
    )(tile_expert, n_used, xs, wg, wu, wd)


def _combine_kernel(dest_ref, y_ref, wts_ref, x1_ref, mod_ref, gpost_ref, gnext_ref, modn_ref,
                    x2_ref, hn_ref, ybuf, sem, *, T, TMC, D):
    i = pl.program_id(0)
    slot = i % 2

    def gather(tile, dst_slot):
        def row_copy(r, k):
            return pltpu.make_async_copy(y_ref.at[pl.ds(dest_ref[k * T + tile * TMC + r], 1)],
                                         ybuf.at[dst_slot, k, pl.ds(r, 1)], sem.at[dst_slot])
        return row_copy

    @pl.when(i == 0)
    def _():
        _row_dma_start(gather(0, 0), TMC)

    @pl.when(i + 1 < pl.num_programs(0))
    def _():
        _row_dma_start(gather(i + 1, 1 - slot), TMC)

    _row_dma_wait(gather(i, slot), TMC)
    w = wts_ref[...]
    y = w[:, 0:1] * ybuf[slot, 0] + w[:, 1:2] * ybuf[slot, 1]
    _ffn_epilogue(y, x1_ref, mod_ref, gpost_ref, gnext_ref, modn_ref, x2_ref, hn_ref, D)


def _combine(dest_flat, y, wts, x1, mod, gpost, gnext, modn, cfg):
    T, D = x1.shape
    TMC = cfg["TM_MERGE"]
    row = lambda i, d: (i, 0)
    const = lambda i, d: (0, 0)
    modrow = lambda i, d: (cfg["mod_row"](i, TMC), 0, 0)
    return pl.pallas_call(
        functools.partial(_combine_kernel, T=T, TMC=TMC, D=D),
        grid_spec=pltpu.PrefetchScalarGridSpec(
            num_scalar_prefetch=1,
            grid=(T // TMC,),
            in_specs=[pl.BlockSpec(memory_space=pl.ANY),
                      pl.BlockSpec((TMC, TOP_K), row),
                      pl.BlockSpec((TMC, D), row),
                      pl.BlockSpec((1, 1, 6 * D), modrow),
                      pl.BlockSpec((1, D), const), pl.BlockSpec((1, D), const),
                      pl.BlockSpec((1, 1, 6 * D), modrow)],
            out_specs=[pl.BlockSpec((TMC, D), row), pl.BlockSpec((TMC, D), row)],
            scratch_shapes=[pltpu.VMEM((2, TOP_K, TMC, D), F32), pltpu.SemaphoreType.DMA((2,))]),
        out_shape=[jax.ShapeDtypeStruct((T, D), F32), jax.ShapeDtypeStruct((T, D), BF16)],
        compiler_params=_cp(("arbitrary",)),
        name="moe_combine",
    )(dest_flat, y, wts, x1, mod, gpost, gnext, modn)


def _rope_tables(Ss, TM):
    n_freq = LANES // 4
    t = jnp.arange(Ss, dtype=I32)
    rows = (t // GRID_W).astype(F32)
    cols = (t % GRID_W).astype(F32)
    inv_freq = jnp.power(ROPE_THETA, -jnp.arange(n_freq, dtype=F32) / n_freq)
    ar = rows[:, None] * inv_freq
    ac = cols[:, None] * inv_freq
    cos = jnp.concatenate([jnp.cos(ar), jnp.cos(ar), jnp.cos(ac), jnp.cos(ac)], axis=-1)
    sin = jnp.concatenate([-jnp.sin(ar), jnp.sin(ar), -jnp.sin(ac), jnp.sin(ac)], axis=-1)
    cos = jnp.concatenate([jnp.ones((TM, LANES), F32), cos], axis=0)
    sin = jnp.concatenate([jnp.zeros((TM, LANES), F32), sin], axis=0)
    return cos, sin


def _dft_tables(n):
    j = jnp.arange(n, dtype=I32)
    ph = (j[:, None] * j[None, :]) % n
    ang = ph.astype(F32) * (2.0 * math.pi / n)
    return jnp.cos(ang).astype(BF16), jnp.sin(ang).astype(BF16)


def kernel(x_prompt, x_sample, cache_k, cache_v, c, c_ctx, w_mod, b_mod, g_pre_mix, g_post_mix,
           g_pre_ffn, g_post_ffn, w_in, g_q, g_k, w_pool_map, pool_scale, w_attn_o, w_pool_o,
           w_four_o, w_out, w_ffn_gate, w_ffn_up, w_ffn_down, w_router, w_exp_gate, w_exp_up,
           w_exp_down):
    Bp, Sp, D = x_prompt.shape
    Bs, Ss, _ = x_sample.shape
    L = w_mod.shape[0]
    PAST, KV = cache_k.shape[2], cache_k.shape[3]
    AW, PW, FW = w_attn_o.shape[1], w_pool_o.shape[1], w_four_o.shape[1]
    NH = AW // LANES
    G = NH // KV
    KW = KV * LANES
    E = w_router.shape[2]
    NP, NS = Bp * Sp, Bs * Ss
    T = NP + NS
    TM = min(512, math.gcd(NP, Ss))
    assert cache_k.shape[4] == LANES and NP % TM == 0 and Ss % TM == 0 and NP % Ss == 0
    assert Ss % Sp == 0 and Sp & (Sp - 1) == 0 and Ss & (Ss - 1) == 0 and Ss % GRID_W == 0
    NPT, TPS = NP // TM, Ss // TM
    TM_BIG = min(1024, math.gcd(NP, Ss))
    TME = TM
    P = (-(-(T * TOP_K) // TME) + E) * TME

    pick_tile = lambda n, cands: next(t for t in cands if n % t == 0)
    cfg = dict(
        TM=TM, TM_MERGE=min(TM, 256), TME=TME, TM_FFN=TM, TM_GATES=TM_BIG,
        TF=pick_tile(w_ffn_gate.shape[2], (512, 256, 128)),
        TFE=pick_tile(w_exp_gate.shape[3], (1024, 512, 256, 128)),
        NH=NH, KV=KV, PW=PW, FW=FW, NP=NP, Sp=Sp, Ss=Ss,
        mod_row=lambda i, tm=TM: jnp.where(i * tm < NP, 0, 1 + (i * tm - NP) // Ss),
        rope_blk=lambda i: jnp.where(i < NPT, 0, 1 + (i - NPT) % TPS),
    )

    x = jnp.concatenate([x_prompt.reshape(NP, D), x_sample.reshape(NS, D)], axis=0)
    R = -(-(1 + Bs) // 8) * 8
    cond = jnp.zeros((R, D), F32).at[0].set(c_ctx).at[1:1 + Bs].set(c)
    mod_all = _modulation(cond, w_mod, b_mod).reshape(L, R, 1, 6 * D)

    o1, o2, o3, o4 = AW, AW + 2 * KW, AW + 2 * KW + PW, AW + 2 * KW + PW + FW
    w_in_b = w_in.astype(BF16)
    cos_tab, sin_tab = _rope_tables(Ss, TM)
    cc, sc = _dft_tables(LANES)
    cs_chan = jnp.concatenate([cc, sc], axis=1)
    dft_p = _dft_tables(Sp)
    dft_s = _dft_tables(Ss)
    ck = cache_k.astype(BF16).reshape(Bs, L, PAST, KW)
    cv = cache_v.astype(BF16).reshape(Bs, L, PAST, KW)
    row1 = lambda a: a.reshape(1, -1)
    dense_w = [w.astype(BF16) for w in (w_ffn_gate, w_ffn_up, w_ffn_down)]
    expert_w = [w.astype(BF16) for w in (w_exp_gate, w_exp_up, w_exp_down)]

    h = _norm_mod(x, row1(g_pre_mix[0]), mod_all[0], cfg)
    new_k, new_v = [], []
    for l in range(L):
        mod = mod_all[l]
        q, kb, vb, kf, vf = _qkv(h, w_in_b[l, :, :o2], row1(g_q[l]), row1(g_k[l]), cos_tab, sin_tab, cfg)
        u_pool, fa, fb = _pool_fourier_proj(h, w_in_b[l, :, o2:o4], cs_chan, cfg)
        gates = _gates(h, w_in_b[l, :, o4:], cfg)
        new_k.append(kf[:NP].reshape(Bp, Sp, KV, LANES))
        new_v.append(vf[:NP].reshape(Bp, Sp, KV, LANES))

        attn_p = _attention(q, kb, vb, q_row0=0, B=Bp, S=Sp, Tk=Sp, KV=KV, G=G)
        k_s = jnp.concatenate([ck[:, l], kb[NP:].reshape(Bs, Ss, KW)], axis=1).reshape(-1, KW)
        v_s = jnp.concatenate([cv[:, l], vb[NP:].reshape(Bs, Ss, KW)], axis=1).reshape(-1, KW)
        attn_s = _attention(q, k_s, v_s, q_row0=NP, B=Bs, S=Ss, Tk=PAST + Ss, KV=KV, G=G)

        pool = _pool(u_pool, w_pool_map[l].astype(BF16), row1(pool_scale[l]), cfg)
        four_p = _fourier(fa, fb, *dft_p, row0=0, B=Bp, S=Sp)
        four_s = _fourier(fa, fb, *dft_s, row0=NP, B=Bs, S=Ss)

        x1, h2 = _merge(attn_p, attn_s, pool, four_p, four_s, gates, x, mod, w_attn_o[l].astype(BF16),
                        w_pool_o[l].astype(BF16), w_four_o[l].astype(BF16), w_out[l].astype(BF16),
                        row1(g_post_mix[l]), row1(g_pre_ffn[l]), BF16 if l % 2 == 0 else F32, cfg)

        ln = min(l + 1, L - 1)
        gnext, modn = row1(g_pre_mix[ln]), mod_all[ln]
        i = l // 2
        if l % 2 == 0:
            x, h = _dense_ffn(h2, *dense_w, i, x1, mod, row1(g_post_ffn[l]), gnext, modn, cfg)
        else:
            eidx, rank, wts, cnt = _router(h2, w_router[i].T, cfg)
            counts = cnt[:, 0]
            padded = (counts + TME - 1) // TME * TME
            pad_end = jnp.cumsum(padded)
            pad_start = pad_end - padded
            start_of = sum(jnp.where(eidx == e, pad_start[e], 0) for e in range(E))
            dest = (start_of + rank).reshape(-1).astype(I32)
            n_used = (pad_end[-1] // TME).astype(I32).reshape(1)
            tile_start = jnp.arange(P // TME, dtype=I32) * TME
            tile_expert = jnp.minimum(
                jnp.sum(tile_start[:, None] >= pad_end[None, :], axis=1), E - 1).astype(I32)
            pad_lo = jnp.concatenate([pad_start + counts, pad_end[-1:]]).astype(I32)
            pad_hi = jnp.concatenate([pad_end, jnp.full((1,), P, pad_end.dtype)]).astype(I32)
            xs = _dispatch(dest, pad_lo, pad_hi, h2, P, cfg)
            y = _moe_ffn(tile_expert, n_used, xs, *expert_w, i, cfg)
            x, h = _combine(dest, y, wts.T, x1, mod, row1(g_post_ffn[l]), gnext, modn, cfg)

    y_prompt = x[:NP].reshape(Bp, Sp, D)
    y_sample = x[NP:].reshape(Bs, Ss, D)
    return (y_prompt, y_sample, jnp.stack(new_k, axis=1), jnp.stack(new_v, axis=1))
```

```python
import functools
import math

import jax
import jax.numpy as jnp
from jax import lax
from jax.experimental import pallas as pl
from jax.experimental.pallas import tpu as pltpu

F32 = jnp.float32
BF16 = jnp.bfloat16
I32 = jnp.int32

LANES = 128
GRID_W = 64
ROPE_THETA = 10000.0
EPS = 1e-6
TOP_K = 2
POOL_WINDOWS = (2, 4, 8, 16)
VMEM_LIMIT = 56 * 1024 * 1024


def _cp(sem, vmem=VMEM_LIMIT):
    return pltpu.CompilerParams(dimension_semantics=sem, vmem_limit_bytes=vmem)


def _rms(x, g):
    return x * lax.rsqrt(jnp.mean(x * x, axis=-1, keepdims=True) + EPS) * g


def _silu(x):
    return x / (1.0 + jnp.exp(-x))


def _dot(a, b):
    return jnp.dot(a, b, preferred_element_type=F32)


def _resident(shape, index_map):
    return pl.BlockSpec(shape, index_map, pipeline_mode=pl.Buffered(1))


def _mod_kernel(c_ref, w_ref, b_ref, o_ref):
    s = _silu(c_ref[...])
    o_ref[0] = jnp.dot(s, w_ref[0], preferred_element_type=F32,
                       precision=lax.Precision.HIGHEST) + b_ref[0]


def _modulation(cond, w_mod, b_mod):
    L, D, N = w_mod.shape
    R = cond.shape[0]
    tn = 1024
    return pl.pallas_call(
        _mod_kernel,
        grid=(L, N // tn),
        in_specs=[pl.BlockSpec((R, D), lambda l, j: (0, 0)),
                  pl.BlockSpec((1, D, tn), lambda l, j: (l, 0, j)),
                  pl.BlockSpec((1, 1, tn), lambda l, j: (l, 0, j))],
        out_specs=pl.BlockSpec((1, R, tn), lambda l, j: (l, 0, j)),
        out_shape=jax.ShapeDtypeStruct((L, R, N), F32),
        compiler_params=_cp(("arbitrary", "arbitrary")),
        name="modulation",
    )(cond, w_mod, b_mod.reshape(L, 1, N))


def _norm_mod_kernel(x_ref, g_ref, mod_ref, h_ref, *, D):
    mod = mod_ref[0]
    y = _rms(x_ref[...], g_ref[...])
    h_ref[...] = (y * (1.0 + mod[:, D:2 * D]) + mod[:, 0:D]).astype(BF16)


def _norm_mod(x, g, mod, cfg):
    T, D = x.shape
    TM = cfg["TM"]
    return pl.pallas_call(
        functools.partial(_norm_mod_kernel, D=D),
        grid=(T // TM,),
        in_specs=[pl.BlockSpec((TM, D), lambda i: (i, 0)),
                  pl.BlockSpec((1, D), lambda i: (0, 0)),
                  pl.BlockSpec((1, 1, 6 * D), lambda i: (cfg["mod_row"](i), 0, 0))],
        out_specs=pl.BlockSpec((TM, D), lambda i: (i, 0)),
        out_shape=jax.ShapeDtypeStruct((T, D), BF16),
        compiler_params=_cp(("arbitrary",)),
        name="norm_mod",
    )(x, g, mod)


def _qkv_kernel(h_ref, w_ref, gq_ref, gk_ref, cos_ref, sin_ref,
                q_ref, kb_ref, vb_ref, kf_ref, vf_ref, *, NH, KV, scale):
    acc = _dot(h_ref[...], w_ref[...])
    cos = cos_ref[...]
    sin = sin_ref[...]
    lane = lax.broadcasted_iota(I32, cos.shape, 1)
    first_half = (lane % (LANES // 2)) < (LANES // 4)

    def rope(y):
        partner = jnp.where(first_half, pltpu.roll(y, LANES - LANES // 4, axis=1),
                            pltpu.roll(y, LANES // 4, axis=1))
        return y * cos + partner * sin

    for hd in range(NH):
        z = acc[:, hd * LANES:(hd + 1) * LANES]
        q_ref[:, hd * LANES:(hd + 1) * LANES] = (rope(_rms(z, gq_ref[...])) * scale).astype(BF16)
    for hd in range(KV):
        z = acc[:, (NH + hd) * LANES:(NH + hd + 1) * LANES]
        y = _rms(z, gk_ref[...])
        kf_ref[:, hd * LANES:(hd + 1) * LANES] = y
        kb_ref[:, hd * LANES:(hd + 1) * LANES] = rope(y).astype(BF16)
    v = acc[:, (NH + KV) * LANES:(NH + 2 * KV) * LANES]
    vf_ref[...] = v
    vb_ref[...] = v.astype(BF16)


def _qkv(h, w, gq, gk, cos_tab, sin_tab, cfg):
    T, D = h.shape
    TM, NH, KV = cfg["TM"], cfg["NH"], cfg["KV"]
    AW, KW = NH * LANES, KV * LANES
    row = lambda i: (i, 0)
    return pl.pallas_call(
        functools.partial(_qkv_kernel, NH=NH, KV=KV, scale=float(LANES) ** -0.5 * math.log2(math.e)),
        grid=(T // TM,),
        in_specs=[pl.BlockSpec((TM, D), row),
                  _resident((D, AW + 2 * KW), lambda i: (0, 0)),
                  pl.BlockSpec((1, LANES), lambda i: (0, 0)),
                  pl.BlockSpec((1, LANES), lambda i: (0, 0)),
                  pl.BlockSpec((TM, LANES), lambda i: (cfg["rope_blk"](i), 0)),
                  pl.BlockSpec((TM, LANES), lambda i: (cfg["rope_blk"](i), 0))],
        out_specs=[pl.BlockSpec((TM, AW), row), pl.BlockSpec((TM, KW), row),
                   pl.BlockSpec((TM, KW), row), pl.BlockSpec((TM, KW), row),
                   pl.BlockSpec((TM, KW), row)],
        out_shape=[jax.ShapeDtypeStruct((T, AW), BF16), jax.ShapeDtypeStruct((T, KW), BF16),
                   jax.ShapeDtypeStruct((T, KW), BF16), jax.ShapeDtypeStruct((T, KW), F32),
                   jax.ShapeDtypeStruct((T, KW), F32)],
        compiler_params=_cp(("arbitrary",)),
        name="qkv_proj",
    )(h, w, gq, gk, cos_tab, sin_tab)


def _pf_kernel(h_ref, w_ref, cs_ref, up_ref, a_ref, b_ref, *, PW, NG):
    acc = _dot(h_ref[...], w_ref[...])
    up_ref[...] = acc[:, :PW]
    for g in range(NG):
        u = acc[:, PW + g * LANES:PW + (g + 1) * LANES].astype(BF16)
        ab = _dot(u, cs_ref[...])
        a_ref[:, g * LANES:(g + 1) * LANES] = ab[:, :LANES].astype(BF16)
        b_ref[:, g * LANES:(g + 1) * LANES] = ab[:, LANES:].astype(BF16)


def _pool_fourier_proj(h, w, cs, cfg):
    T, D = h.shape
    TM, PW, FW = cfg["TM"], cfg["PW"], cfg["FW"]
    row = lambda i: (i, 0)
    return pl.pallas_call(
        functools.partial(_pf_kernel, PW=PW, NG=FW // LANES),
        grid=(T // TM,),
        in_specs=[pl.BlockSpec((TM, D), row),
                  _resident((D, PW + FW), lambda i: (0, 0)),
                  pl.BlockSpec((LANES, 2 * LANES), lambda i: (0, 0))],
        out_specs=[pl.BlockSpec((TM, PW), row), pl.BlockSpec((TM, FW), row),
                   pl.BlockSpec((TM, FW), row)],
        out_shape=[jax.ShapeDtypeStruct((T, PW), F32), jax.ShapeDtypeStruct((T, FW), BF16),
                   jax.ShapeDtypeStruct((T, FW), BF16)],
        compiler_params=_cp(("arbitrary",)),
        name="pool_fourier_proj",
    )(h, w, cs)


def _gates_kernel(h_ref, w_ref, o_ref):
    z = _dot(h_ref[...], w_ref[...])
    o_ref[...] = (0.5 * jnp.tanh(0.5 * z) + 0.5).astype(BF16)


def _gates(h, w, cfg):
    T, D = h.shape
    N = w.shape[1]
    TM = cfg["TM_GATES"]
    tn = 2048 if N % 2048 == 0 else 1024
    return pl.pallas_call(
        _gates_kernel,
        grid=(N // tn, T // TM),
        in_specs=[pl.BlockSpec((TM, D), lambda j, i: (i, 0)),
                  pl.BlockSpec((D, tn), lambda j, i: (0, j))],
        out_specs=pl.BlockSpec((TM, tn), lambda j, i: (i, j)),
        out_shape=jax.ShapeDtypeStruct((T, N), BF16),
        compiler_params=_cp(("arbitrary", "arbitrary")),
        name="gates_proj",
    )(h, w)


def _attn_kernel(q_ref, k_ref, v_ref, o_ref, *, G, tq, tk, nk):
    q = jnp.concatenate([q_ref[:, g * LANES:(g + 1) * LANES] for g in range(G)], axis=0)
    rows = G * tq
    nt = tk // LANES
    m = jnp.full((rows, LANES), -jnp.inf, F32)
    l = jnp.zeros((rows, LANES), F32)
    acc = jnp.zeros((rows, LANES), F32)
    for c in range(nk):
        k = k_ref[c * tk:(c + 1) * tk, :]
        v = v_ref[c * tk:(c + 1) * tk, :]
        s = lax.dot_general(q, k, (((1,), (1,)), ((), ())), preferred_element_type=F32)
        m_new = jnp.maximum(m, jnp.max(s, axis=-1, keepdims=True))
        alpha = jnp.exp2(m - m_new)
        p = [jnp.exp2(s[:, t * LANES:(t + 1) * LANES] - m_new) for t in range(nt)]
        l = alpha * l + functools.reduce(lambda a, b: a + b, p)
        pb = jnp.concatenate([pt.astype(BF16) for pt in p], axis=1)
        acc = alpha * acc + _dot(pb, v)
        m = m_new
    o = acc / jnp.sum(l, axis=-1, keepdims=True)
    for g in range(G):
        o_ref[:, g * LANES:(g + 1) * LANES] = o[g * tq:(g + 1) * tq].astype(BF16)


def _attention(q, k, v, *, q_row0, B, S, Tk, KV, G):
    tq = min(256, S)
    tk = next(t for t in (512, 256, 128) if Tk % t == 0)
    nq, nk = S // tq, Tk // tk
    qb0 = q_row0 // tq
    return pl.pallas_call(
        functools.partial(_attn_kernel, G=G, tq=tq, tk=tk, nk=nk),
        grid=(B, KV, nq),
        in_specs=[pl.BlockSpec((tq, G * LANES), lambda b, h, qi: (qb0 + b * nq + qi, h)),
                  pl.BlockSpec((Tk, LANES), lambda b, h, qi: (b, h)),
                  pl.BlockSpec((Tk, LANES), lambda b, h, qi: (b, h))],
        out_specs=pl.BlockSpec((tq, G * LANES), lambda b, h, qi: (b * nq + qi, h)),
        out_shape=jax.ShapeDtypeStruct((B * S, KV * G * LANES), BF16),
        compiler_params=_cp(("arbitrary",) * 3),
        name="attention",
    )(q, k, v)


def _pool_kernel(u_ref, wmap_ref, scale_ref, o_ref, tot_sc, *, R, n_prompt_blocks, Sp, Ss):
    blk = pl.program_id(0)
    g = pl.program_id(1)
    S = jnp.where(blk < n_prompt_blocks, Sp, Ss)
    half = jnp.left_shift(1, g)
    pos = lax.broadcasted_iota(I32, (R, LANES), 0) & (S - 1)

    def add_offsets(offsets):
        u = u_ref[...]
        tot = tot_sc[...]
        for k in offsets:
            valid = jnp.logical_and(pos + k >= 0, pos + k < S)
            tot = tot + jnp.where(valid, pltpu.roll(u, (-k) % R, axis=0), 0.0)
        tot_sc[...] = tot

    tot_sc[...] = u_ref[...]
    add_offsets([-1])
    for lvl in range(1, len(POOL_WINDOWS)):
        h0, h1 = POOL_WINDOWS[lvl - 1] // 2, POOL_WINDOWS[lvl] // 2

        @pl.when(g >= lvl)
        def _():
            add_offsets(list(range(-h1, -h0)) + list(range(h0, h1)))

    cnt = (jnp.minimum(pos + half, S) - jnp.maximum(pos - half, 0)).astype(F32)
    pooled = tot_sc[...] / cnt - u_ref[...]
    mixed = _dot(pooled.astype(BF16), wmap_ref[0]) * scale_ref[...]
    o_ref[...] = mixed.astype(BF16)


def _pool(u, wmap, scale, cfg):
    T, PW = u.shape
    R, NP, Sp, Ss = cfg["Ss"], cfg["NP"], cfg["Sp"], cfg["Ss"]
    return pl.pallas_call(
        functools.partial(_pool_kernel, R=R, n_prompt_blocks=NP // R, Sp=Sp, Ss=Ss),
        grid=(T // R, PW // LANES),
        in_specs=[pl.BlockSpec((R, LANES), lambda b, g: (b, g)),
                  pl.BlockSpec((1, LANES, LANES), lambda b, g: (g, 0, 0)),
                  pl.BlockSpec((1, LANES), lambda b, g: (0, g))],
        out_specs=pl.BlockSpec((R, LANES), lambda b, g: (b, g)),
        out_shape=jax.ShapeDtypeStruct((T, PW), BF16),
        scratch_shapes=[pltpu.VMEM((R, LANES), F32)],
        compiler_params=_cp(("arbitrary", "arbitrary")),
        name="multiscale_pool",
    )(u, wmap, scale)


def _fourier_kernel(cs_ref, ss_ref, a_ref, b_ref, o_ref, acc_sc, *, scale):
    k = pl.program_id(2)

    @pl.when(k == 0)
    def _():
        acc_sc[...] = jnp.zeros(acc_sc.shape, F32)

    acc_sc[...] += _dot(cs_ref[...], a_ref[...]) - _dot(ss_ref[...], b_ref[...])

    @pl.when(k == pl.num_programs(2) - 1)
    def _():
        o_ref[...] = (acc_sc[...] * scale).astype(BF16)


def _fourier(a, b, cs, ss, *, row0, B, S):
    FW = a.shape[1]
    tm = min(1024, S)
    tk = min(1024, S)
    nm, nk = S // tm, S // tk
    kb0 = row0 // tk
    return pl.pallas_call(
        functools.partial(_fourier_kernel, scale=float(S * LANES) ** -0.5),
        grid=(B, nm, nk),
        in_specs=[pl.BlockSpec((tm, tk), lambda bb, i, k: (i, k)),
                  pl.BlockSpec((tm, tk), lambda bb, i, k: (i, k)),
                  pl.BlockSpec((tk, FW), lambda bb, i, k: (kb0 + bb * nk + k, 0)),
                  pl.BlockSpec((tk, FW), lambda bb, i, k: (kb0 + bb * nk + k, 0))],
        out_specs=pl.BlockSpec((tm, FW), lambda bb, i, k: (bb * nm + i, 0)),
        out_shape=jax.ShapeDtypeStruct((B * S, FW), BF16),
        scratch_shapes=[pltpu.VMEM((tm, FW), F32)],
        compiler_params=_cp(("arbitrary",) * 3),
        name="fourier_mix",
    )(cs, ss, a, b)


def _merge_kernel(attn_p_ref, attn_s_ref, pool_ref, four_p_ref, four_s_ref, gates_ref, x_ref, mod_ref,
                  wa_ref, wp_ref, wf_ref, wo_ref, gpost_ref, gpre_ref, x1_ref, h2_ref, *, D, n_prompt_tiles):
    mod = mod_ref[0]
    is_prompt = pl.program_id(0) < n_prompt_tiles
    attn = jnp.where(is_prompt, attn_p_ref[...], attn_s_ref[...])
    four = jnp.where(is_prompt, four_p_ref[...], four_s_ref[...])
    merged = (gates_ref[:, 0:D].astype(F32) * _dot(attn, wa_ref[...])
              + gates_ref[:, D:2 * D].astype(F32) * _dot(pool_ref[...], wp_ref[...])
              + gates_ref[:, 2 * D:3 * D].astype(F32) * _dot(four, wf_ref[...]))
    y = _dot(merged.astype(BF16), wo_ref[...])
    x1 = x_ref[...] + mod[:, 2 * D:3 * D] * _rms(y, gpost_ref[...])
    x1_ref[...] = x1
    h2 = _rms(x1, gpre_ref[...]) * (1.0 + mod[:, 4 * D:5 * D]) + mod[:, 3 * D:4 * D]
    h2_ref[...] = h2.astype(h2_ref.dtype)


def _merge(attn_p, attn_s, pool, four_p, four_s, gates, x, mod, wa, wp, wf, wo, gpost, gpre, h2_dtype, cfg):
    T, D = x.shape
    TM = cfg["TM_MERGE"]
    npt = cfg["NP"] // TM
    row = lambda i: (i, 0)
    const = lambda i: (0, 0)
    prow = lambda i: (jnp.minimum(i, npt - 1), 0)
    srow = lambda i: (jnp.maximum(i - npt, 0), 0)
    AW, FW = attn_p.shape[1], four_p.shape[1]
    return pl.pallas_call(
        functools.partial(_merge_kernel, D=D, n_prompt_tiles=npt),
        grid=(T // TM,),
        in_specs=[pl.BlockSpec((TM, AW), prow), pl.BlockSpec((TM, AW), srow),
                  pl.BlockSpec((TM, pool.shape[1]), row),
                  pl.BlockSpec((TM, FW), prow), pl.BlockSpec((TM, FW), srow),
                  pl.BlockSpec((TM, 3 * D), row),
                  pl.BlockSpec((TM, D), row),
                  pl.BlockSpec((1, 1, 6 * D), lambda i: (cfg["mod_row"](i, TM), 0, 0)),
                  _resident(wa.shape, const), _resident(wp.shape, const),
                  _resident(wf.shape, const), _resident(wo.shape, const),
                  pl.BlockSpec((1, D), const), pl.BlockSpec((1, D), const)],
        out_specs=[pl.BlockSpec((TM, D), row), pl.BlockSpec((TM, D), row)],
        out_shape=[jax.ShapeDtypeStruct((T, D), F32), jax.ShapeDtypeStruct((T, D), h2_dtype)],
        compiler_params=_cp(("arbitrary",)),
        name="merge_out_proj",
    )(attn_p, attn_s, pool, four_p, four_s, gates, x, mod, wa, wp, wf, wo, gpost, gpre)


def _ffn_epilogue(y, x1_ref, mod_ref, gpost_ref, gnext_ref, modn_ref, x2_ref, hn_ref, D):
    mod = mod_ref[0]
    x2 = x1_ref[...] + mod[:, 5 * D:6 * D] * _rms(y, gpost_ref[...])
    x2_ref[...] = x2
    modn = modn_ref[0]
    hn = _rms(x2, gnext_ref[...]) * (1.0 + modn[:, D:2 * D]) + modn[:, 0:D]
    hn_ref[...] = hn.astype(BF16)


def _swiglu_mid(x, wg_ref, wu_ref):
    return (_silu(_dot(x, wg_ref[...])) * _dot(x, wu_ref[...])).astype(BF16)


def _ffn_kernel(h_ref, wg_ref, wu_ref, wd_ref, x1_ref, mod_ref, gpost_ref, gnext_ref, modn_ref,
                x2_ref, hn_ref, acc_sc, *, D):
    f = pl.program_id(1)

    @pl.when(f == 0)
    def _():
        acc_sc[...] = jnp.zeros(acc_sc.shape, F32)

    acc_sc[...] += _dot(_swiglu_mid(h_ref[...], wg_ref, wu_ref), wd_ref[...])

    @pl.when(f == pl.num_programs(1) - 1)
    def _():
        _ffn_epilogue(acc_sc[...], x1_ref, mod_ref, gpost_ref, gnext_ref, modn_ref, x2_ref, hn_ref, D)


def _dense_ffn(h, wg, wu, wd, layer, x1, mod, gpost, gnext, modn, cfg):
    T, D = h.shape
    DFF = wd.shape[1]
    TM = cfg["TM_FFN"]
    tf = cfg["TF"]
    row = lambda i, f: (i, 0)
    const = lambda i, f: (0, 0)
    modrow = lambda i, f: (cfg["mod_row"](i, TM), 0, 0)
    return pl.pallas_call(
        functools.partial(_ffn_kernel, D=D),
        grid=(T // TM, DFF // tf),
        in_specs=[pl.BlockSpec((TM, D), row),
                  pl.BlockSpec((None, D, tf), lambda i, f: (layer, 0, f)),
                  pl.BlockSpec((None, D, tf), lambda i, f: (layer, 0, f)),
                  pl.BlockSpec((None, tf, D), lambda i, f: (layer, f, 0)),
                  pl.BlockSpec((TM, D), row),
                  pl.BlockSpec((1, 1, 6 * D), modrow),
                  pl.BlockSpec((1, D), const), pl.BlockSpec((1, D), const),
                  pl.BlockSpec((1, 1, 6 * D), modrow)],
        out_specs=[pl.BlockSpec((TM, D), row), pl.BlockSpec((TM, D), row)],
        out_shape=[jax.ShapeDtypeStruct((T, D), F32), jax.ShapeDtypeStruct((T, D), BF16)],
        scratch_shapes=[pltpu.VMEM((TM, D), F32)],
        compiler_params=_cp(("arbitrary", "arbitrary")),
        name="dense_ffn",
    )(h, wg, wu, wd, x1, mod, gpost, gnext, modn)


def _router_kernel(h_ref, wr_ref, eidx_ref, rank_ref, wts_ref, cnt_ref, base_sc, *, E, TMR):
    i = pl.program_id(0)

    @pl.when(i == 0)
    def _():
        base_sc[...] = jnp.zeros(base_sc.shape, F32)

    logits = lax.dot_general(wr_ref[...], h_ref[...].astype(F32), (((1,), (1,)), ((), ())),
                             preferred_element_type=F32, precision=lax.Precision.HIGHEST)
    e_iota = lax.broadcasted_iota(I32, (E, TMR), 0)
    m1 = jnp.max(logits, axis=0, keepdims=True)
    i1 = jnp.min(jnp.where(logits == m1, e_iota, E), axis=0, keepdims=True)
    rest = jnp.where(e_iota == i1, -jnp.inf, logits)
    m2 = jnp.max(rest, axis=0, keepdims=True)
    i2 = jnp.min(jnp.where(rest == m2, e_iota, E), axis=0, keepdims=True)
    ex = jnp.exp(m2 - m1)
    w1 = 1.0 / (1.0 + ex)
    w2 = ex / (1.0 + ex)

    oh1 = e_iota == i1
    oh2 = e_iota == i2
    oh = jnp.where(jnp.logical_or(oh1, oh2), 1.0, 0.0)
    tri = jnp.where(lax.broadcasted_iota(I32, (TMR, TMR), 0) < lax.broadcasted_iota(I32, (TMR, TMR), 1),
                    1.0, 0.0).astype(BF16)
    before = _dot(oh.astype(BF16), tri) + base_sc[...]
    r1 = jnp.sum(jnp.where(oh1, before, 0.0), axis=0, keepdims=True)
    r2 = jnp.sum(jnp.where(oh2, before, 0.0), axis=0, keepdims=True)
    base_sc[...] = base_sc[...] + jnp.sum(oh, axis=1, keepdims=True)

    eidx_ref[0:1, :] = i1
    eidx_ref[1:2, :] = i2
    rank_ref[0:1, :] = r1.astype(I32)
    rank_ref[1:2, :] = r2.astype(I32)
    wts_ref[0:1, :] = w1
    wts_ref[1:2, :] = w2
    cnt_ref[...] = jnp.broadcast_to(base_sc[...], cnt_ref.shape).astype(I32)


def _router(h, wr_t, cfg):
    T, D = h.shape
    E = wr_t.shape[0]
    TMR = cfg["TM"]
    col = lambda i: (0, i)
    return pl.pallas_call(
        functools.partial(_router_kernel, E=E, TMR=TMR),
        grid=(T // TMR,),
        in_specs=[pl.BlockSpec((TMR, D), lambda i: (i, 0)),
                  pl.BlockSpec((E, D), lambda i: (0, 0))],
        out_specs=[pl.BlockSpec((TOP_K, TMR), col), pl.BlockSpec((TOP_K, TMR), col),
                   pl.BlockSpec((TOP_K, TMR), col), pl.BlockSpec((E, LANES), lambda i: (0, 0))],
        out_shape=[jax.ShapeDtypeStruct((TOP_K, T), I32), jax.ShapeDtypeStruct((TOP_K, T), I32),
                   jax.ShapeDtypeStruct((TOP_K, T), F32), jax.ShapeDtypeStruct((E, LANES), I32)],
        scratch_shapes=[pltpu.VMEM((E, 1), F32)],
        compiler_params=_cp(("arbitrary",)),
        name="moe_router",
    )(h, wr_t)


ROW_DMA_UNROLL = 8


def _row_dma_start(row_copy, n_rows):
    def start(rb, c):
        for u in range(ROW_DMA_UNROLL):
            for k in range(TOP_K):
                row_copy(rb * ROW_DMA_UNROLL + u, k).start(priority=(u * TOP_K + k) % 2)
        return c

    lax.fori_loop(0, n_rows // ROW_DMA_UNROLL, start, 0)


def _row_dma_wait(row_copy, n_rows):
    def wait(rb, c):
        for u in range(ROW_DMA_UNROLL):
            for k in range(TOP_K):
                row_copy(rb * ROW_DMA_UNROLL + u, k).wait()
        return c

    lax.fori_loop(0, n_rows // ROW_DMA_UNROLL, wait, 0)


def _dispatch_kernel(dest_ref, pad_lo_ref, pad_hi_ref, h_ref, xs_ref, zero_row, sem, pad_sem,
                     *, T, TMD, n_pad_ranges):
    base = pl.program_id(0) * TMD

    @pl.when(pl.program_id(0) == 0)
    def _():
        zero_row[...] = jnp.zeros(zero_row.shape, zero_row.dtype)

        def pad_copy(r):
            return pltpu.make_async_copy(zero_row, xs_ref.at[pl.ds(r, 1)], pad_sem)

        def start(r, c):
            pad_copy(r).start()
            return c

        def wait(r, c):
            pad_copy(r).wait()
            return c

        for z in range(n_pad_ranges):
            lax.fori_loop(pad_lo_ref[z], pad_hi_ref[z], start, 0)
        for z in range(n_pad_ranges):
            lax.fori_loop(pad_lo_ref[z], pad_hi_ref[z], wait, 0)

    def row_copy(r, k):
        return pltpu.make_async_copy(h_ref.at[pl.ds(r, 1)],
                                     xs_ref.at[pl.ds(dest_ref[k * T + base + r], 1)], sem)

    _row_dma_start(row_copy, TMD)
    _row_dma_wait(row_copy, TMD)


def _dispatch(dest_flat, pad_lo, pad_hi, h, P, cfg):
    T, D = h.shape
    TMD = cfg["TM"]
    return pl.pallas_call(
        functools.partial(_dispatch_kernel, T=T, TMD=TMD, n_pad_ranges=pad_lo.shape[0]),
        grid_spec=pltpu.PrefetchScalarGridSpec(
            num_scalar_prefetch=3,
            grid=(T // TMD,),
            in_specs=[pl.BlockSpec((TMD, D), lambda i, dest, lo, hi: (i, 0))],
            out_specs=pl.BlockSpec(memory_space=pl.ANY),
            scratch_shapes=[pltpu.VMEM((1, D), h.dtype), pltpu.SemaphoreType.DMA(()),
                            pltpu.SemaphoreType.DMA(())]),
        out_shape=jax.ShapeDtypeStruct((P, D), h.dtype),
        compiler_params=pltpu.CompilerParams(dimension_semantics=("arbitrary",),
                                             has_side_effects=True),
        name="moe_dispatch",
    )(dest_flat, pad_lo, pad_hi, h)


def _moe_kernel(te_ref, nu_ref, xs_ref, wg_ref, wu_ref, wd_ref, y_ref, acc_sc, xb_sc):
    j = pl.program_id(0)
    f = pl.program_id(1)
    used = j < nu_ref[0]
    last = f == pl.num_programs(1) - 1

    @pl.when(jnp.logical_and(used, f == 0))
    def _():
        acc_sc[...] = jnp.zeros(acc_sc.shape, F32)
        xb_sc[...] = xs_ref[...].astype(BF16)

    @pl.when(used)
    def _():
        acc_sc[...] += _dot(_swiglu_mid(xb_sc[...], wg_ref, wu_ref), wd_ref[...])

    @pl.when(jnp.logical_and(used, last))
    def _():
        y_ref[...] = acc_sc[...]

    @pl.when(jnp.logical_and(jnp.logical_not(used), last))
    def _():
        y_ref[...] = jnp.zeros(y_ref.shape, F32)


def _moe_ffn(tile_expert, n_used, xs, wg, wu, wd, layer, cfg):
    P, D = xs.shape
    DFE = wd.shape[2]
    TME, tf = cfg["TME"], cfg["TFE"]
    nf = DFE // tf

    def jj(j, nu):
        return jnp.minimum(j, nu[0] - 1)

    def ff(j, f, nu):
        return jnp.where(j < nu[0], f, nf - 1)

    return pl.pallas_call(
        _moe_kernel,
        grid_spec=pltpu.PrefetchScalarGridSpec(
            num_scalar_prefetch=2,
            grid=(P // TME, nf),
            in_specs=[pl.BlockSpec((TME, D), lambda j, f, te, nu: (jj(j, nu), 0)),
                      pl.BlockSpec((None, None, D, tf),
                                   lambda j, f, te, nu: (layer, te[jj(j, nu)], 0, ff(j, f, nu))),
                      pl.BlockSpec((None, None, D, tf),
                                   lambda j, f, te, nu: (layer, te[jj(j, nu)], 0, ff(j, f, nu))),
                      pl.BlockSpec((None, None, tf, D),
                                   lambda j, f, te, nu: (layer, te[jj(j, nu)], ff(j, f, nu), 0))],
            out_specs=pl.BlockSpec((TME, D), lambda j, f, te, nu: (j, 0)),
            scratch_shapes=[pltpu.VMEM((TME, D), F32), pltpu.VMEM((TME, D), BF16)]),
        out_shape=jax.ShapeDtypeStruct((P, D), F32),
        compiler_params=_cp(("arbitrary", "arbitrary")),
        name="moe_expert_ffn",
    )(tile_expert, n_used, xs, wg, wu, wd)


def _combine_kernel(dest_ref, y_ref, wts_ref, x1_ref, mod_ref, gpost_ref, gnext_ref, modn_ref,
                    x2_ref, hn_ref, ybuf, sem, *, T, TMC, D):
    i = pl.program_id(0)
    slot = i % 2

    def gather(tile, dst_slot):
        def row_copy(r, k):
            return pltpu.make_async_copy(y_ref.at[pl.ds(dest_ref[k * T + tile * TMC + r], 1)],
                                         ybuf.at[dst_slot, k, pl.ds(r, 1)], sem.at[dst_slot])
        return row_copy

    @pl.when(i == 0)
    def _():
        _row_dma_start(gather(0, 0), TMC)

    @pl.when(i + 1 < pl.num_programs(0))
    def _():
        _row_dma_start(gather(i + 1, 1 - slot), TMC)

    _row_dma_wait(gather(i, slot), TMC)
    w = wts_ref[...]
    y = w[:, 0:1] * ybuf[slot, 0] + w[:, 1:2] * ybuf[slot, 1]
    _ffn_epilogue(y, x1_ref, mod_ref, gpost_ref, gnext_ref, modn_ref, x2_ref, hn_ref, D)


def _combine(dest_flat, y, wts, x1, mod, gpost, gnext, modn, cfg):
    T, D = x1.shape
    TMC = cfg["TM_MERGE"]
    row = lambda i, d: (i, 0)
    const = lambda i, d: (0, 0)
    modrow = lambda i, d: (cfg["mod_row"](i, TMC), 0, 0)
    return pl.pallas_call(
        functools.partial(_combine_kernel, T=T, TMC=TMC, D=D),
        grid_spec=pltpu.PrefetchScalarGridSpec(
            num_scalar_prefetch=1,
            grid=(T // TMC,),
            in_specs=[pl.BlockSpec(memory_space=pl.ANY),
                      pl.BlockSpec((TMC, TOP_K), row),
                      pl.BlockSpec((TMC, D), row),
                      pl.BlockSpec((1, 1, 6 * D), modrow),
                      pl.BlockSpec((1, D), const), pl.BlockSpec((1, D), const),
                      pl.BlockSpec((1, 1, 6 * D), modrow)],
            out_specs=[pl.BlockSpec((TMC, D), row), pl.BlockSpec((TMC, D), row)],
            scratch_shapes=[pltpu.VMEM((2, TOP_K, TMC, D), F32), pltpu.SemaphoreType.DMA((2,))]),
        out_shape=[jax.ShapeDtypeStruct((T, D), F32), jax.ShapeDtypeStruct((T, D), BF16)],
        compiler_params=_cp(("arbitrary",)),
        name="moe_combine",
    )(dest_flat, y, wts, x1, mod, gpost, gnext, modn)


def _rope_tables(Ss, TM):
    n_freq = LANES // 4
    t = jnp.arange(Ss, dtype=I32)
    rows = (t // GRID_W).astype(F32)
    cols = (t % GRID_W).astype(F32)
    inv_freq = jnp.power(ROPE_THETA, -jnp.arange(n_freq, dtype=F32) / n_freq)
    ar = rows[:, None] * inv_freq
    ac = cols[:, None] * inv_freq
    cos = jnp.concatenate([jnp.cos(ar), jnp.cos(ar), jnp.cos(ac), jnp.cos(ac)], axis=-1)
    sin = jnp.concatenate([-jnp.sin(ar), jnp.sin(ar), -jnp.sin(ac), jnp.sin(ac)], axis=-1)
    cos = jnp.concatenate([jnp.ones((TM, LANES), F32), cos], axis=0)
    sin = jnp.concatenate([jnp.zeros((TM, LANES), F32), sin], axis=0)
    return cos, sin


def _dft_tables(n):
    j = jnp.arange(n, dtype=I32)
    ph = (j[:, None] * j[None, :]) % n
    ang = ph.astype(F32) * (2.0 * math.pi / n)
    return jnp.cos(ang).astype(BF16), jnp.sin(ang).astype(BF16)


def kernel(x_prompt, x_sample, cache_k, cache_v, c, c_ctx, w_mod, b_mod, g_pre_mix, g_post_mix,
           g_pre_ffn, g_post_ffn, w_in, g_q, g_k, w_pool_map, pool_scale, w_attn_o, w_pool_o,
           w_four_o, w_out, w_ffn_gate, w_ffn_up, w_ffn_down, w_router, w_exp_gate, w_exp_up,
           w_exp_down):
    Bp, Sp, D = x_prompt.shape
    Bs, Ss, _ = x_sample.shape
    L = w_mod.shape[0]
    PAST, KV = cache_k.shape[2], cache_k.shape[3]
    AW, PW, FW = w_attn_o.shape[1], w_pool_o.shape[1], w_four_o.shape[1]
    NH = AW // LANES
    G = NH // KV
    KW = KV * LANES
    E = w_router.shape[2]
    NP, NS = Bp * Sp, Bs * Ss
    T = NP + NS
    TM = min(512, math.gcd(NP, Ss))
    assert cache_k.shape[4] == LANES and NP % TM == 0 and Ss % TM == 0 and NP % Ss == 0
    assert Ss % Sp == 0 and Sp & (Sp - 1) == 0 and Ss & (Ss - 1) == 0 and Ss % GRID_W == 0
    NPT, TPS = NP // TM, Ss // TM
    TM_BIG = min(1024, math.gcd(NP, Ss))
    TME = TM
    P = (-(-(T * TOP_K) // TME) + E) * TME

    pick_tile = lambda n, cands: next(t for t in cands if n % t == 0)
    cfg = dict(
        TM=TM, TM_MERGE=min(TM, 256), TME=TME, TM_FFN=TM, TM_GATES=TM_BIG,
        TF=pick_tile(w_ffn_gate.shape[2], (512, 256, 128)),
        TFE=pick_tile(w_exp_gate.shape[3], (1024, 512, 256, 128)),
        NH=NH, KV=KV, PW=PW, FW=FW, NP=NP, Sp=Sp, Ss=Ss,
        mod_row=lambda i, tm=TM: jnp.where(i * tm < NP, 0, 1 + (i * tm - NP) // Ss),
        rope_blk=lambda i: jnp.where(i < NPT, 0, 1 + (i - NPT) % TPS),
    )

    x = jnp.concatenate([x_prompt.reshape(NP, D), x_sample.reshape(NS, D)], axis=0)
    R = -(-(1 + Bs) // 8) * 8
    cond = jnp.zeros((R, D), F32).at[0].set(c_ctx).at[1:1 + Bs].set(c)
    mod_all = _modulation(cond, w_mod, b_mod).reshape(L, R, 1, 6 * D)

    o1, o2, o3, o4 = AW, AW + 2 * KW, AW + 2 * KW + PW, AW + 2 * KW + PW + FW
    w_in_b = w_in.astype(BF16)
    cos_tab, sin_tab = _rope_tables(Ss, TM)
    cc, sc = _dft_tables(LANES)
    cs_chan = jnp.concatenate([cc, sc], axis=1)
    dft_p = _dft_tables(Sp)
    dft_s = _dft_tables(Ss)
    ck = cache_k.astype(BF16).reshape(Bs, L, PAST, KW)
    cv = cache_v.astype(BF16).reshape(Bs, L, PAST, KW)
    row1 = lambda a: a.reshape(1, -1)
    dense_w = [w.astype(BF16) for w in (w_ffn_gate, w_ffn_up, w_ffn_down)]
    expert_w = [w.astype(BF16) for w in (w_exp_gate, w_exp_up, w_exp_down)]

    h = _norm_mod(x, row1(g_pre_mix[0]), mod_all[0], cfg)
    new_k, new_v = [], []
    for l in range(L):
        mod = mod_all[l]
        q, kb, vb, kf, vf = _qkv(h, w_in_b[l, :, :o2], row1(g_q[l]), row1(g_k[l]), cos_tab, sin_tab, cfg)
        u_pool, fa, fb = _pool_fourier_proj(h, w_in_b[l, :, o2:o4], cs_chan, cfg)
        gates = _gates(h, w_in_b[l, :, o4:], cfg)
        new_k.append(kf[:NP].reshape(Bp, Sp, KV, LANES))
        new_v.append(vf[:NP].reshape(Bp, Sp, KV, LANES))

        attn_p = _attention(q, kb, vb, q_row0=0, B=Bp, S=Sp, Tk=Sp, KV=KV, G=G)
        k_s = jnp.concatenate([ck[:, l], kb[NP:].reshape(Bs, Ss, KW)], axis=1).reshape(-1, KW)
        v_s = jnp.concatenate([cv[:, l], vb[NP:].reshape(Bs, Ss, KW)], axis=1).reshape(-1, KW)
        attn_s = _attention(q, k_s, v_s, q_row0=NP, B=Bs, S=Ss, Tk=PAST + Ss, KV=KV, G=G)

        pool = _pool(u_pool, w_pool_map[l].astype(BF16), row1(pool_scale[l]), cfg)
        four_p = _fourier(fa, fb, *dft_p, row0=0, B=Bp, S=Sp)
        four_s = _fourier(fa, fb, *dft_s, row0=NP, B=Bs, S=Ss)

        x1, h2 = _merge(attn_p, attn_s, pool, four_p, four_s, gates, x, mod, w_attn_o[l].astype(BF16),
                        w_pool_o[l].astype(BF16), w_four_o[l].astype(BF16), w_out[l].astype(BF16),
                        row1(g_post_mix[l]), row1(g_pre_ffn[l]), BF16 if l % 2 == 0 else F32, cfg)

        ln = min(l + 1, L - 1)
        gnext, modn = row1(g_pre_mix[ln]), mod_all[ln]
        i = l // 2
        if l % 2 == 0:
            x, h = _dense_ffn(h2, *dense_w, i, x1, mod, row1(g_post_ffn[l]), gnext, modn, cfg)
        else:
            eidx, rank, wts, cnt = _router(h2, w_router[i].T, cfg)
            counts = cnt[:, 0]
            padded = (counts + TME - 1) // TME * TME
            pad_end = jnp.cumsum(padded)
            pad_start = pad_end - padded
            start_of = sum(jnp.where(eidx == e, pad_start[e], 0) for e in range(E))
            dest = (start_of + rank).reshape(-1).astype(I32)
            n_used = (pad_end[-1] // TME).astype(I32).reshape(1)
            tile_start = jnp.arange(P // TME, dtype=I32) * TME
            tile_expert = jnp.minimum(
                jnp.sum(tile_start[:, None] >= pad_end[None, :], axis=1), E - 1).astype(I32)
            pad_lo = jnp.concatenate([pad_start + counts, pad_end[-1:]]).astype(I32)
            pad_hi = jnp.concatenate([pad_end, jnp.full((1,), P, pad_end.dtype)]).astype(I32)
            xs = _dispatch(dest, pad_lo, pad_hi, h2, P, cfg)
            y = _moe_ffn(tile_expert, n_used, xs, *expert_w, i, cfg)
            x, h = _combine(dest, y, wts.T, x1, mod, row1(g_post_ffn[l]), gnext, modn, cfg)

    y_prompt = x[:NP].reshape(Bp, Sp, D)
    y_sample = x[NP:].reshape(Bs, Ss, D)
    return (y_prompt, y_sample, jnp.stack(new_k, axis=1), jnp.stack(new_v, axis=1))
```
